```python
import math
import jax, jax.numpy as jnp
from jax import lax
import numpy as np

D_MODEL = 1024
BATCH = 8
SEQ = 4096
DEPTH = 1

MIX_WIDTH = D_MODEL
HG_HEADS = 4
HG_DK = 128
HG_DV = 128
HG_KEY_WIDTH = HG_HEADS * HG_DK
HG_VAL_WIDTH = HG_HEADS * HG_DV
HG_CHUNK = 64
ATT_HEADS = 8
ATT_KV_HEADS = 2
ATT_GROUP = ATT_HEADS // ATT_KV_HEADS
ATT_HD = 64
ATT_Q_WIDTH = ATT_HEADS * ATT_HD
ATT_KV_WIDTH = ATT_KV_HEADS * ATT_HD
WINDOW = 128
ATT_BLOCK = 128
HG_COLS = 2 * HG_KEY_WIDTH + 2 * HG_VAL_WIDTH
ATT_COLS = ATT_Q_WIDTH + 2 * ATT_KV_WIDTH
IN_COLS = HG_COLS + ATT_COLS
D_FF = 2816
CONV_W = 3
EPS = 1e-6

kernel_name = "hymba_hgrn2_swa_sink_convffn"


def rmsnorm(x, w, eps=EPS):
    xf = x.astype(jnp.float32)
    inv = lax.rsqrt(jnp.mean(xf * xf, axis=-1, keepdims=True) + eps)
    return (xf * inv * w.astype(jnp.float32)).astype(x.dtype)


def hgrn2_chunkwise(q, k, v, log_f):
    B, S, H, DK = q.shape
    DV = v.shape[-1]
    n = S // HG_CHUNK

    def to_chunks(a):
        return a.astype(jnp.float32).reshape(B, n, HG_CHUNK, H, a.shape[-1]).transpose(1, 0, 3, 2, 4)

    qc, kc, vc, gc = to_chunks(q), to_chunks(k), to_chunks(v), to_chunks(log_f)
    causal = jnp.tril(jnp.ones((HG_CHUNK, HG_CHUNK), dtype=bool))[:, :, None]

    def step(state, inp):
        qi, ki, vi, gi = inp
        b = jnp.cumsum(gi, axis=2)
        rel = b[:, :, :, None, :] - b[:, :, None, :, :]
        decay = jnp.exp(jnp.where(causal, rel, -jnp.inf))
        scores = jnp.einsum('bhtk,bhsk,bhtsk->bhts', qi, ki, decay)
        intra = jnp.einsum('bhts,bhsv->bhtv', scores, vi)
        inter = jnp.einsum('bhtk,bhkv->bhtv', qi * jnp.exp(b), state)
        b_last = b[:, :, -1:, :]
        new_state = state * jnp.exp(b_last)[:, :, 0, :, None] + jnp.einsum(
            'bhsk,bhsv->bhkv', ki * jnp.exp(b_last - b), vi)
        return new_state, intra + inter

    state0 = jnp.zeros((B, H, DK, DV), jnp.float32)
    _, out = lax.scan(step, state0, (qc, kc, vc, gc))
    return out.transpose(1, 0, 3, 2, 4).reshape(B, S, H, DV)


def sliding_window_attention_with_sinks(q, k, v, sinks):
    B, S = q.shape[0], q.shape[1]
    nb = S // ATT_BLOCK
    scale = 1.0 / math.sqrt(ATT_HD)
    qb = q.astype(jnp.float32).reshape(B, nb, ATT_BLOCK, ATT_KV_HEADS, ATT_GROUP, ATT_HD)

    def band_keys(a):
        ap = jnp.pad(a.astype(jnp.float32), ((0, 0), (ATT_BLOCK, 0), (0, 0), (0, 0)))
        ap = ap.reshape(B, nb + 1, ATT_BLOCK, ATT_KV_HEADS, ATT_HD)
        return jnp.concatenate([ap[:, :-1], ap[:, 1:]], axis=2)

    kb, vb = band_keys(k), band_keys(v)
    scores = jnp.einsum('bnqhgd,bnkhd->bnhgqk', qb, kb) * scale
    qi = jnp.arange(ATT_BLOCK)[:, None]
    kj = jnp.arange(2 * ATT_BLOCK)[None, :]
    dist = qi + ATT_BLOCK - kj
    band = (dist >= 0) & (dist < WINDOW)
    key_pos = jnp.arange(nb)[:, None] * ATT_BLOCK + jnp.arange(2 * ATT_BLOCK)[None, :] - ATT_BLOCK
    mask = band[None] & (key_pos >= 0)[:, None, :]
    scores = jnp.where(mask[None, :, None, None], scores, -jnp.inf)
    sink = sinks.astype(jnp.float32).reshape(ATT_KV_HEADS, ATT_GROUP)[None, None, :, :, None, None]
    m = jnp.maximum(jnp.max(scores, axis=-1, keepdims=True), sink)
    p = jnp.exp(scores - m)
    denom = jnp.sum(p, axis=-1, keepdims=True) + jnp.exp(sink - m)
    out = jnp.einsum('bnhgqk,bnkhd->bnqhgd', p / denom, vb)
    return out.reshape(B, S, ATT_HEADS * ATT_HD)


def causal_depthwise_conv(a, w, b):
    C = a.shape[-1]
    y = lax.conv_general_dilated(
        a, w[:, None, :].astype(a.dtype), window_strides=(1,), padding=[(CONV_W - 1, 0)],
        dimension_numbers=('NWC', 'WIO', 'NWC'), feature_group_count=C)
    return y + b.astype(a.dtype)


def setup_inputs(seed: int = 0) -> dict:
    key = jax.random.key(seed)
    ks = jax.random.split(key, 16)
    f32 = jnp.float32
    nrm = lambda k, shape, s: jax.random.normal(k, shape, f32) * s
    return {
        "x": jax.random.normal(ks[0], (BATCH, SEQ, D_MODEL), f32),
        "norm_mix_w": 1.0 + nrm(ks[1], (DEPTH, D_MODEL), 0.02),
        "w_in": nrm(ks[2], (DEPTH, D_MODEL, IN_COLS), D_MODEL ** -0.5),
        "b_attn": nrm(ks[3], (DEPTH, ATT_COLS), 0.02),
        "lb_logits": nrm(ks[4], (DEPTH + 1, HG_KEY_WIDTH), 0.1),
        "hg_norm_w": 1.0 + nrm(ks[5], (DEPTH, HG_DV), 0.02),
        "sinks": nrm(ks[6], (DEPTH, ATT_HEADS), 0.5),
        "w_out": nrm(ks[7], (DEPTH, MIX_WIDTH, D_MODEL), MIX_WIDTH ** -0.5),
        "norm_ffn_w": 1.0 + nrm(ks[8], (DEPTH, D_MODEL), 0.02),
        "w_gate": nrm(ks[9], (DEPTH, D_MODEL, D_FF), D_MODEL ** -0.5),
        "w_up": nrm(ks[10], (DEPTH, D_MODEL, D_FF), D_MODEL ** -0.5),
        "conv_w": nrm(ks[11], (DEPTH, CONV_W, D_FF), CONV_W ** -0.5),
        "conv_b": nrm(ks[12], (DEPTH, D_FF), 0.02),
        "w_down": nrm(ks[13], (DEPTH, D_FF, D_MODEL), D_FF ** -0.5),
        "final_norm_w": 1.0 + nrm(ks[14], (D_MODEL,), 0.02),
    }


def reference(x, norm_mix_w, w_in, b_attn, lb_logits, hg_norm_w, sinks, w_out,
              norm_ffn_w, w_gate, w_up, conv_w, conv_b, w_down, final_norm_w):
    B, S, _ = x.shape
    lb_all = jnp.cumsum(jax.nn.softmax(lb_logits.astype(jnp.float32), axis=0), axis=0)[:DEPTH]
    h = x
    for l in range(DEPTH):
        u = rmsnorm(h, norm_mix_w[l])
        proj = u @ w_in[l]
        hq, hf, hi, hg, att = jnp.split(
            proj, [HG_KEY_WIDTH, 2 * HG_KEY_WIDTH, 2 * HG_KEY_WIDTH + HG_VAL_WIDTH, HG_COLS], axis=-1)
        lb = lb_all[l]
        f = lb + (1.0 - lb) * jax.nn.sigmoid(hf.astype(jnp.float32))
        log_f = jnp.log(f).reshape(B, S, HG_HEADS, HG_DK)
        k_hg = (1.0 - f).reshape(B, S, HG_HEADS, HG_DK)
        q_hg = hq.astype(jnp.float32).reshape(B, S, HG_HEADS, HG_DK) * (HG_DK ** -0.5)
        v_hg = hi.reshape(B, S, HG_HEADS, HG_DV)
        o_hg = hgrn2_chunkwise(q_hg, k_hg, v_hg, log_f)
        o_hg = rmsnorm(o_hg, hg_norm_w[l]).reshape(B, S, HG_VAL_WIDTH)
        o_hg = (o_hg * jax.nn.silu(hg.astype(jnp.float32))).astype(h.dtype)
        att = att + b_attn[l]
        aq, ak, av = jnp.split(att, [ATT_Q_WIDTH, ATT_Q_WIDTH + ATT_KV_WIDTH], axis=-1)
        o_att = sliding_window_attention_with_sinks(
            aq.reshape(B, S, ATT_HEADS, ATT_HD),
            ak.reshape(B, S, ATT_KV_HEADS, ATT_HD),
            av.reshape(B, S, ATT_KV_HEADS, ATT_HD),
            sinks[l]).astype(h.dtype)
        mix = jnp.concatenate([o_hg, o_att], axis=-1)
        h = h + mix @ w_out[l]
        v = rmsnorm(h, norm_ffn_w[l])
        gate = causal_depthwise_conv(v @ w_gate[l], conv_w[l], conv_b[l])
        h = h + (jax.nn.silu(gate) * (v @ w_up[l])) @ w_down[l]
    return rmsnorm(h, final_norm_w)
```

```python
import functools
import math

import jax
import jax.numpy as jnp
from jax import lax
from jax.experimental import pallas as pl
from jax.experimental.pallas import tpu as pltpu

F32 = jnp.float32
BF16 = jnp.bfloat16

D_MODEL = 1024
HG_HEADS = 4
HG_DK = 128
HG_DV = 128
HG_WIDTH = HG_HEADS * HG_DK
ATT_HEADS = 8
ATT_KV_HEADS = 2
ATT_HD = 64
ATT_Q_WIDTH = ATT_HEADS * ATT_HD
ATT_KV_WIDTH = ATT_KV_HEADS * ATT_HD
ATT_BLOCK = 128
WINDOW = 128
HG_COLS = 4 * HG_WIDTH
ATT_COLS = ATT_Q_WIDTH + 2 * ATT_KV_WIDTH
D_FF = 2816
CONV_W = 3
EPS = 1e-6

LANES = 128
SUBLANES = 8
HG_CHUNK = 128
HG_LEVELS = (64, 32, 16, 8)
HG_DIAG = 8
VMEM_LIMIT = 56 * 1024 * 1024


def _dot(a, b):
    return jnp.dot(a, b, preferred_element_type=F32)


def _dot_nt(a, b):
    return lax.dot_general(a, b, (((1,), (1,)), ((), ())), preferred_element_type=F32)


def _dot_tn(a, b):
    return lax.dot_general(a, b, (((0,), (0,)), ((), ())), preferred_element_type=F32)


def _sigmoid(x):
    return 1.0 / (1.0 + jnp.exp(-x))


def _inproj_kernel(x_ref, nw_ref, whg_ref, watt_ref, batt_ref,
                   hq_ref, hf_ref, hi_ref, hg_ref, aq_ref, ak_ref, av_ref):
    x = x_ref[...]
    inv = lax.rsqrt(jnp.mean(x * x, axis=-1, keepdims=True) + EPS)
    u = (x * inv * nw_ref[...]).astype(BF16)
    p = _dot(u, whg_ref[...])
    hq_ref[...] = p[:, 0 * HG_WIDTH:1 * HG_WIDTH]
    hf_ref[...] = p[:, 1 * HG_WIDTH:2 * HG_WIDTH]
    hi_ref[...] = p[:, 2 * HG_WIDTH:3 * HG_WIDTH].astype(BF16)
    hg_ref[...] = p[:, 3 * HG_WIDTH:4 * HG_WIDTH]
    a = _dot(u, watt_ref[...]) + batt_ref[...]
    aq_ref[...] = (a[:, :ATT_Q_WIDTH] * (1.0 / math.sqrt(ATT_HD))).astype(BF16)
    ak_ref[...] = a[:, ATT_Q_WIDTH:ATT_Q_WIDTH + ATT_KV_WIDTH].astype(BF16)
    av_ref[...] = a[:, ATT_Q_WIDTH + ATT_KV_WIDTH:].astype(BF16)


def _inproj(x2, nw, whg, watt, batt, tm):
    n = x2.shape[0]
    row = lambda w: pl.BlockSpec((tm, w), lambda i: (i, 0))
    const = lambda s: pl.BlockSpec(s, lambda i: (0, 0), pipeline_mode=pl.Buffered(1))
    return pl.pallas_call(
        _inproj_kernel,
        grid=(n // tm,),
        in_specs=[row(D_MODEL), const((1, D_MODEL)), const((D_MODEL, HG_COLS)),
                  const((D_MODEL, ATT_COLS)), const((1, ATT_COLS))],
        out_specs=[row(HG_WIDTH), row(HG_WIDTH), row(HG_WIDTH), row(HG_WIDTH),
                   row(ATT_Q_WIDTH), row(ATT_KV_WIDTH), row(ATT_KV_WIDTH)],
        out_shape=[jax.ShapeDtypeStruct((n, HG_WIDTH), F32),
                   jax.ShapeDtypeStruct((n, HG_WIDTH), F32),
                   jax.ShapeDtypeStruct((n, HG_WIDTH), BF16),
                   jax.ShapeDtypeStruct((n, HG_WIDTH), F32),
                   jax.ShapeDtypeStruct((n, ATT_Q_WIDTH), BF16),
                   jax.ShapeDtypeStruct((n, ATT_KV_WIDTH), BF16),
                   jax.ShapeDtypeStruct((n, ATT_KV_WIDTH), BF16)],
        compiler_params=pltpu.CompilerParams(
            dimension_semantics=("arbitrary",), vmem_limit_bytes=VMEM_LIMIT),
        name="inproj",
    )(x2, nw, whg, watt, batt)


def _split3(g):
    g1 = g.astype(BF16)
    r1 = g - g1.astype(F32)
    g2 = r1.astype(BF16)
    g3 = (r1 - g2.astype(F32)).astype(BF16)
    return g1, g2, g3


def _hgrn2_head(q, hf, v, hg, lb, nw, st_ref, h, tri):
    C = HG_CHUNK
    f = lb + (1.0 - lb) * _sigmoid(hf)
    g = jnp.log(f)
    kk = 1.0 - f
    q = q * (HG_DK ** -0.5)
    g1, g2, g3 = _split3(g)
    b = _dot(tri, g3) + _dot(tri, g2) + _dot(tri, g1)

    row = lax.broadcasted_iota(jnp.int32, (C, LANES), 0)
    srow = lax.broadcasted_iota(jnp.int32, (C, C), 0)
    scol = lax.broadcasted_iota(jnp.int32, (C, C), 1)
    neg = jnp.float32(-jnp.inf)

    scores = jnp.zeros((C, C), F32)
    for L in HG_LEVELS:
        nb = C // (2 * L)
        r = jnp.broadcast_to(b.reshape(nb, 2 * L, LANES)[:, L - 1:L, :],
                             (nb, 2 * L, LANES)).reshape(C, LANES)
        in_q = (row % (2 * L)) >= L
        qh = (q * jnp.exp(jnp.where(in_q, b - r, neg))).astype(BF16)
        kh = (kk * jnp.exp(jnp.where(in_q, neg, r - b))).astype(BF16)
        s = _dot_nt(qh, kh)
        if nb > 1:
            s = jnp.where((srow // (2 * L)) == (scol // (2 * L)), s, 0.0)
        scores = scores + s

    nd = C // HG_DIAG
    b3 = b.reshape(nd, HG_DIAG, LANES)
    q3 = q.reshape(nd, HG_DIAG, LANES)
    k3 = kk.reshape(nd, HG_DIAG, LANES)
    toff = lax.broadcasted_iota(jnp.int32, (nd, HG_DIAG, LANES), 1)
    blk = lax.broadcasted_iota(jnp.int32, (nd, HG_DIAG, LANES), 0)
    lane = lax.broadcasted_iota(jnp.int32, (nd, HG_DIAG, LANES), 2)
    sd = jnp.zeros((nd, HG_DIAG, LANES), F32)
    for so in range(HG_DIAG):
        bs = b3[:, so:so + 1, :]
        ks = k3[:, so:so + 1, :]
        a = q3 * ks * jnp.exp(jnp.where(toff >= so, b3 - bs, neg))
        p = jnp.sum(a, axis=-1, keepdims=True)
        sd = jnp.where(lane == blk * HG_DIAG + so, p, sd)
    scores = scores + sd.reshape(C, C)

    st = st_ref[h]
    o = _dot(scores.astype(BF16), v)
    o = o + _dot_nt((q * jnp.exp(b)).astype(BF16), st.astype(BF16))
    bl = b[C - 1:C, :]
    kt = (kk * jnp.exp(bl - b)).astype(BF16)
    st_ref[h] = st * jnp.exp(bl) + _dot_tn(v, kt)

    inv = lax.rsqrt(jnp.mean(o * o, axis=-1, keepdims=True) + EPS)
    o = o * inv * nw
    return (o * (hg * _sigmoid(hg))).astype(BF16)


def _hgrn2_kernel(hq_ref, hf_ref, hi_ref, hg_ref, lbl_ref, nw_ref, o_ref, st_ref):
    @pl.when(pl.program_id(1) == 0)
    def _():
        st_ref[...] = jnp.zeros_like(st_ref)

    l = lbl_ref[...]
    e = jnp.exp(l - jnp.max(l, axis=0, keepdims=True))
    lb_all = e[0:1, :] / jnp.sum(e, axis=0, keepdims=True)
    nw = nw_ref[...]
    C = HG_CHUNK
    tri = (lax.broadcasted_iota(jnp.int32, (C, C), 0)
           >= lax.broadcasted_iota(jnp.int32, (C, C), 1)).astype(BF16)
    for h in range(HG_HEADS):
        sl = slice(h * HG_DK, (h + 1) * HG_DK)
        o_ref[:, sl] = _hgrn2_head(hq_ref[:, sl], hf_ref[:, sl], hi_ref[:, sl], hg_ref[:, sl],
                                   lb_all[:, sl], nw, st_ref, h, tri)


def _hgrn2(hq, hf, hi, hg, lbl, nw, batch, seq):
    n = hq.shape[0]
    C = HG_CHUNK
    nc = seq // C
    row = pl.BlockSpec((C, HG_WIDTH), lambda b, c: (b * nc + c, 0))
    return pl.pallas_call(
        _hgrn2_kernel,
        grid=(batch, nc),
        in_specs=[row, row, row, row,
                  pl.BlockSpec(lbl.shape, lambda b, c: (0, 0)),
                  pl.BlockSpec((1, HG_DV), lambda b, c: (0, 0))],
        out_specs=row,
        out_shape=jax.ShapeDtypeStruct((n, HG_WIDTH), BF16),
        scratch_shapes=[pltpu.VMEM((HG_HEADS, HG_DV, HG_DK), F32)],
        compiler_params=pltpu.CompilerParams(
            dimension_semantics=("arbitrary", "arbitrary"), vmem_limit_bytes=VMEM_LIMIT),
        name="hgrn2",
    )(hq, hf, hi, hg, lbl, nw)


def _attn_kernel(sink_ref, aq_ref, akp_ref, akc_ref, avp_ref, avc_ref, o_ref):
    nblk = pl.program_id(1)
    T = ATT_BLOCK
    G = ATT_HEADS // ATT_KV_HEADS
    aq = aq_ref[...]
    qs = jnp.concatenate([aq[:, c * LANES:(c + 1) * LANES] for c in range(G)], axis=0)
    k = jnp.concatenate([akp_ref[...], akc_ref[...]], axis=0)
    v = jnp.concatenate([avp_ref[...], avc_ref[...]], axis=0)
    lane = lax.broadcasted_iota(jnp.int32, (2 * T, LANES), 1)
    zero = jnp.zeros_like(k)
    lo = lane < ATT_HD
    ri = lax.broadcasted_iota(jnp.int32, (G * T, 2 * T), 0) % T
    ci = lax.broadcasted_iota(jnp.int32, (G * T, 2 * T), 1)
    dist = ri + T - ci
    valid = (dist >= 0) & (dist < WINDOW) & ((ci >= T) | (nblk > 0))
    out = jnp.zeros((G * T, LANES), F32)
    for g in range(ATT_KV_HEADS):
        sel = lo if g == 0 else jnp.logical_not(lo)
        kg = jnp.where(sel, k, zero)
        vg = jnp.where(sel, v, zero)
        s = jnp.where(valid, _dot_nt(qs, kg), -jnp.inf)
        sink = jnp.concatenate(
            [jnp.full((T, 1), sink_ref[g * G + c], F32) for c in range(G)], axis=0)
        m = jnp.maximum(jnp.max(s, axis=-1, keepdims=True), sink)
        p = jnp.exp(s - m)
        denom = jnp.sum(p, axis=-1, keepdims=True) + jnp.exp(sink - m)
        out = out + _dot(p.astype(BF16), vg) / denom
    for c in range(G):
        o_ref[:, c * LANES:(c + 1) * LANES] = out[c * T:(c + 1) * T, :].astype(BF16)


def _attention(sinks, aq, ak, av, batch, seq):
    n = aq.shape[0]
    T = ATT_BLOCK
    nb = seq // T
    cur = lambda w: pl.BlockSpec((T, w), lambda b, i: (b * nb + i, 0))
    prev = lambda w: pl.BlockSpec((T, w), lambda b, i: (b * nb + jnp.maximum(i - 1, 0), 0))
    return pl.pallas_call(
        _attn_kernel,
        grid=(batch, nb),
        in_specs=[pl.BlockSpec(memory_space=pltpu.SMEM),
                  cur(ATT_Q_WIDTH), prev(ATT_KV_WIDTH), cur(ATT_KV_WIDTH),
                  prev(ATT_KV_WIDTH), cur(ATT_KV_WIDTH)],
        out_specs=cur(ATT_Q_WIDTH),
        out_shape=jax.ShapeDtypeStruct((n, ATT_Q_WIDTH), BF16),
        compiler_params=pltpu.CompilerParams(
            dimension_semantics=("arbitrary", "arbitrary"), vmem_limit_bytes=VMEM_LIMIT),
        name="swa",
    )(sinks, aq, ak, ak, av, av)


FF_CHUNKS = ((0, 1024), (1024, 1024), (2048, 768))


def _ffn_kernel(x_ref, ohg_ref, oatt_ref, wo1_ref, wo2_ref, nfw_ref, wg_ref, wu_ref,
                cw_ref, cb_ref, wd_ref, fnw_ref, o_ref, tail_ref):
    tm = x_ref.shape[0]

    @pl.when(pl.program_id(1) == 0)
    def _():
        tail_ref[...] = jnp.zeros_like(tail_ref)

    h = x_ref[...] + _dot(ohg_ref[...], wo1_ref[...]) + _dot(oatt_ref[...], wo2_ref[...])
    inv = lax.rsqrt(jnp.mean(h * h, axis=-1, keepdims=True) + EPS)
    v = (h * inv * nfw_ref[...]).astype(BF16)
    y = jnp.zeros((tm, D_MODEL), F32)
    for (c0, cw) in FF_CHUNKS:
        cs = slice(c0, c0 + cw)
        gp = _dot(v, wg_ref[:, cs])
        up = _dot(v, wu_ref[:, cs])
        gx = jnp.concatenate([tail_ref[:, cs], gp], axis=0)
        tail_ref[:, cs] = gp[tm - SUBLANES:, :]
        gate = (cw_ref[0:1, cs] * gx[SUBLANES - 2:SUBLANES - 2 + tm, :]
                + cw_ref[1:2, cs] * gx[SUBLANES - 1:SUBLANES - 1 + tm, :]
                + cw_ref[2:3, cs] * gp) + cb_ref[:, cs]
        act = (gate * _sigmoid(gate) * up).astype(BF16)
        y = y + _dot(act, wd_ref[cs, :])
    h2 = h + y
    inv2 = lax.rsqrt(jnp.mean(h2 * h2, axis=-1, keepdims=True) + EPS)
    o_ref[...] = h2 * inv2 * fnw_ref[...]


def _ffn(x2, ohg, oatt, wo1, wo2, nfw, wg, wu, cw, cb, wd, fnw, batch, seq, tm):
    n = x2.shape[0]
    nt = seq // tm
    row = lambda w: pl.BlockSpec((tm, w), lambda b, i: (b * nt + i, 0))
    const = lambda s: pl.BlockSpec(s, lambda b, i: (0, 0), pipeline_mode=pl.Buffered(1))
    return pl.pallas_call(
        _ffn_kernel,
        grid=(batch, nt),
        in_specs=[row(D_MODEL), row(HG_WIDTH), row(ATT_Q_WIDTH),
                  const(wo1.shape), const(wo2.shape), const(nfw.shape),
                  const(wg.shape), const(wu.shape), const(cw.shape), const(cb.shape),
                  const(wd.shape), const(fnw.shape)],
        out_specs=row(D_MODEL),
        out_shape=jax.ShapeDtypeStruct((n, D_MODEL), F32),
        scratch_shapes=[pltpu.VMEM((SUBLANES, D_FF), F32)],
        compiler_params=pltpu.CompilerParams(
            dimension_semantics=("arbitrary", "arbitrary"), vmem_limit_bytes=VMEM_LIMIT),
        name="ffn",
    )(x2, ohg, oatt, wo1, wo2, nfw, wg, wu, cw, cb, wd, fnw)


def _head_perm():
    half = ATT_HEADS // ATT_KV_HEADS
    idx = []
    for c in range(half):
        idx += list(range(c * ATT_HD, (c + 1) * ATT_HD))
        idx += list(range((half + c) * ATT_HD, (half + c + 1) * ATT_HD))
    return jnp.asarray(idx, jnp.int32)


def kernel(x, norm_mix_w, w_in, b_attn, lb_logits, hg_norm_w, sinks, w_out, norm_ffn_w,
           w_gate, w_up, conv_w, conv_b, w_down, final_norm_w):
    batch, seq, _ = x.shape
    n = batch * seq
    x2 = x.reshape(n, D_MODEL)
    perm = _head_perm()

    w_in0 = w_in[0]
    whg = w_in0[:, :HG_COLS].astype(BF16)
    wq = w_in0[:, HG_COLS:HG_COLS + ATT_Q_WIDTH][:, perm]
    watt = jnp.concatenate([wq, w_in0[:, HG_COLS + ATT_Q_WIDTH:]], axis=1).astype(BF16)
    ba = b_attn[0]
    batt = jnp.concatenate([ba[:ATT_Q_WIDTH][perm], ba[ATT_Q_WIDTH:]])[None, :]
    wo = w_out[0]
    wo1 = wo[:HG_WIDTH].astype(BF16)
    wo2 = wo[HG_WIDTH:][perm].astype(BF16)

    hq, hf, hi, hg, aq, ak, av = _inproj(x2, norm_mix_w[0][None, :], whg, watt, batt, tm=512)
    ohg = _hgrn2(hq, hf, hi, hg, lb_logits, hg_norm_w[0][None, :], batch, seq)
    oatt = _attention(sinks[0], aq, ak, av, batch, seq)
    out = _ffn(x2, ohg, oatt, wo1, wo2, norm_ffn_w[0][None, :],
               w_gate[0].astype(BF16), w_up[0].astype(BF16), conv_w[0], conv_b[0][None, :],
               w_down[0].astype(BF16), final_norm_w[None, :], batch, seq, tm=512)
    return out.reshape(batch, seq, D_MODEL)
```

```python
import math

import jax
import jax.numpy as jnp
import numpy as np
from jax import lax
from jax.experimental import pallas as pl
from jax.experimental.pallas import tpu as pltpu

F32 = jnp.float32
BF16 = jnp.bfloat16

D_MODEL = 1024
HG_HEADS = 4
HG_DK = 128
HG_DV = 128
HG_WIDTH = HG_HEADS * HG_DK
ATT_HEADS = 8
ATT_KV_HEADS = 2
ATT_GROUP = ATT_HEADS // ATT_KV_HEADS
ATT_HD = 64
ATT_Q_WIDTH = ATT_HEADS * ATT_HD
ATT_KV_WIDTH = ATT_KV_HEADS * ATT_HD
ATT_BLOCK = 128
WINDOW = 128
HG_COLS = 4 * HG_WIDTH
ATT_COLS = ATT_Q_WIDTH + 2 * ATT_KV_WIDTH
D_FF = 2816
CONV_W = 3
EPS = 1e-6
LOG2E = 1.4426950408889634

LANES = 128
SUBLANES = 8
HG_CHUNK = 128
HG_STEP = 256
HG_NV = HG_CHUNK // SUBLANES
HG_LEVELS = (8, 4, 2, 1)
ATT_TQ = 512
VMEM_LIMIT = 56 * 1024 * 1024


def _dot(a, b):
    return jnp.dot(a, b, preferred_element_type=F32)


def _dot_nt(a, b):
    return lax.dot_general(a, b, (((1,), (1,)), ((), ())), preferred_element_type=F32)


def _dot_tn(a, b):
    return lax.dot_general(a, b, (((0,), (0,)), ((), ())), preferred_element_type=F32)


def _sigmoid(x):
    return 1.0 / (1.0 + jnp.exp2(x * (-LOG2E)))


def _inproj_kernel(x_ref, nw_ref, whg_ref, watt_ref, batt_ref,
                   hq_ref, hf_ref, hi_ref, hg_ref, aq_ref, ak_ref, av_ref):
    x = x_ref[...]
    inv = lax.rsqrt(jnp.mean(x * x, axis=-1, keepdims=True) + EPS)
    u = (x * inv * nw_ref[...]).astype(BF16)
    p = _dot(u, whg_ref[...])
    hq_ref[...] = p[:, 0 * HG_WIDTH:1 * HG_WIDTH]
    hf_ref[...] = p[:, 1 * HG_WIDTH:2 * HG_WIDTH]
    hi_ref[...] = p[:, 2 * HG_WIDTH:3 * HG_WIDTH].astype(BF16)
    hg_ref[...] = p[:, 3 * HG_WIDTH:4 * HG_WIDTH]
    a = _dot(u, watt_ref[...]) + batt_ref[...]
    aq_ref[...] = (a[:, :ATT_Q_WIDTH] * (1.0 / math.sqrt(ATT_HD))).astype(BF16)
    ak_ref[...] = a[:, ATT_Q_WIDTH:ATT_Q_WIDTH + ATT_KV_WIDTH].astype(BF16)
    av_ref[...] = a[:, ATT_Q_WIDTH + ATT_KV_WIDTH:].astype(BF16)


def _inproj(x2, nw, whg, watt, batt, tm):
    n = x2.shape[0]
    row = lambda w: pl.BlockSpec((tm, w), lambda i: (i, 0))
    const = lambda s: pl.BlockSpec(s, lambda i: (0, 0), pipeline_mode=pl.Buffered(1))
    return pl.pallas_call(
        _inproj_kernel,
        grid=(n // tm,),
        in_specs=[row(D_MODEL), const((1, D_MODEL)), const((D_MODEL, HG_COLS)),
                  const((D_MODEL, ATT_COLS)), const((1, ATT_COLS))],
        out_specs=[row(HG_WIDTH), row(HG_WIDTH), row(HG_WIDTH), row(HG_WIDTH),
                   row(ATT_Q_WIDTH), row(ATT_KV_WIDTH), row(ATT_KV_WIDTH)],
        out_shape=[jax.ShapeDtypeStruct((n, HG_WIDTH), F32),
                   jax.ShapeDtypeStruct((n, HG_WIDTH), F32),
                   jax.ShapeDtypeStruct((n, HG_WIDTH), BF16),
                   jax.ShapeDtypeStruct((n, HG_WIDTH), F32),
                   jax.ShapeDtypeStruct((n, ATT_Q_WIDTH), BF16),
                   jax.ShapeDtypeStruct((n, ATT_KV_WIDTH), BF16),
                   jax.ShapeDtypeStruct((n, ATT_KV_WIDTH), BF16)],
        compiler_params=pltpu.CompilerParams(
            dimension_semantics=("arbitrary",), vmem_limit_bytes=VMEM_LIMIT),
        name="inproj",
    )(x2, nw, whg, watt, batt)


def _rows(a):
    return [a[m * SUBLANES:(m + 1) * SUBLANES, :] for m in range(HG_NV)]


def _hgrn2_kernel(hq_ref, hf_ref, hi_ref, hg_ref, lbl_ref, nw_ref, tri_ref, sel_ref, o_ref, st_ref):
    C = HG_CHUNK
    nch = HG_STEP // C

    @pl.when(pl.program_id(1) == 0)
    def _():
        st_ref[...] = jnp.zeros_like(st_ref)

    l = lbl_ref[...]
    e = jnp.exp(l - jnp.max(l, axis=0, keepdims=True))
    lb = e[0:1, :] / jnp.sum(e, axis=0, keepdims=True)

    f_all = lb + (1.0 - lb) * _sigmoid(hf_ref[...])
    k_all = 1.0 - f_all
    g = jnp.log(f_all)
    g_hi = g.astype(BF16)
    g_lo = (g - g_hi.astype(F32)).astype(BF16)
    q_all = hq_ref[...]

    row = lax.broadcasted_iota(jnp.int32, (C, C), 0)
    col = lax.broadcasted_iota(jnp.int32, (C, C), 1)
    diag_mask = ((row // SUBLANES) == (col // SUBLANES)) & (col <= row)
    half = C // 2
    urow = lax.broadcasted_iota(jnp.int32, (half, C), 0)
    ucol = lax.broadcasted_iota(jnp.int32, (half, C), 1)
    zero8 = jnp.zeros((SUBLANES, LANES), F32)

    b_ch, scores_off, a_cat = [], [], []
    for ch in range(nch):
        rs_ = slice(ch * C, (ch + 1) * C)
        b_all = _dot(tri_ref[...], jnp.concatenate([g_hi[rs_], g_lo[rs_]], axis=0)) * LOG2E
        b_ch.append(b_all)
        for h in range(HG_HEADS):
            sl = slice(h * HG_DK, (h + 1) * HG_DK)
            b, q, kk, f = b_all[:, sl], q_all[rs_, sl], k_all[rs_, sl], f_all[rs_, sl]
            bv, qv, kv = _rows(b), _rows(q), _rows(kk)
            ev = [jnp.broadcast_to(x[SUBLANES - 1:SUBLANES, :], (SUBLANES, LANES)) for x in bv]
            srows = [None] * HG_NV
            for lv in HG_LEVELS:
                nb = HG_NV // (2 * lv)
                lower = [i * 2 * lv + j for i in range(nb) for j in range(lv)]
                upper = [m + lv for m in lower]
                cat = lambda xs, idx: jnp.concatenate([xs[m] for m in idx], axis=0)
                r = jnp.concatenate([ev[i * 2 * lv + lv - 1] for i in range(nb) for _ in range(lv)], axis=0)
                kh = cat(kv, lower) * jnp.exp2(r - cat(bv, lower))
                qh = cat(qv, upper) * jnp.exp2(cat(bv, upper) - r)
                khv = [kh[j * SUBLANES:(j + 1) * SUBLANES, :] for j in range(half // SUBLANES)]
                kfull = [zero8] * HG_NV
                for j, m in enumerate(lower):
                    kfull[m] = khv[j]
                s = _dot_nt(qh.astype(BF16), jnp.concatenate(kfull, axis=0).astype(BF16))
                if nb > 1:
                    blk = lv * SUBLANES
                    s = jnp.where((urow // blk) == (ucol // (2 * blk)), s, 0.0)
                for j, m in enumerate(upper):
                    piece = s[j * SUBLANES:(j + 1) * SUBLANES, :]
                    srows[m] = piece if srows[m] is None else srows[m] + piece
            srows[0] = jnp.zeros((SUBLANES, C), F32)
            scores_off.append(jnp.concatenate(srows, axis=0))

            f3 = f.reshape(HG_NV, SUBLANES, LANES)
            bd = kk.reshape(HG_NV, SUBLANES, LANES)
            q3 = q.reshape(HG_NV, SUBLANES, LANES)
            lags = [(q3 * bd).astype(BF16)]
            for _ in range(1, SUBLANES):
                bd = f3 * pltpu.roll(bd, 1, axis=1)
                lags.append((q3 * bd).astype(BF16))
            a_cat.append(jnp.concatenate([x.reshape(C, LANES) for x in lags], axis=1))

    nrs = nch * HG_HEADS
    rs = _dot(jnp.concatenate(a_cat, axis=0), sel_ref[...])
    rs = pltpu.roll(rs.reshape(nrs * HG_NV, SUBLANES, LANES), 0, axis=2, stride=1, stride_axis=1)
    rs = rs.reshape(nrs * C, C)

    nw = nw_ref[...]
    for ch in range(nch):
        rs_ = slice(ch * C, (ch + 1) * C)
        for h in range(HG_HEADS):
            sl = slice(h * HG_DK, (h + 1) * HG_DK)
            i = ch * HG_HEADS + h
            b, q, kk = b_ch[ch][:, sl], q_all[rs_, sl], k_all[rs_, sl]
            v = hi_ref[rs_, sl]
            scores = jnp.where(diag_mask, rs[i * C:(i + 1) * C, :], scores_off[i])
            st = st_ref[h]
            o = _dot(scores.astype(BF16), v) + _dot_nt((q * jnp.exp2(b)).astype(BF16), st.astype(BF16))
            bl = b[C - 1:C, :]
            kt = (kk * jnp.exp2(bl - b)).astype(BF16)
            st_ref[h] = st * jnp.exp2(bl) + _dot_tn(v, kt)
            inv = lax.rsqrt(jnp.mean(o * o, axis=-1, keepdims=True) + EPS * HG_DK)
            hg = hg_ref[rs_, sl]
            o_ref[rs_, sl] = (o * inv * nw * (hg * _sigmoid(hg))).astype(BF16)


def _hgrn2_consts():
    C = HG_CHUNK
    t = np.arange(C)
    tri = (t[None, :] <= t[:, None]).astype(np.float32)
    tri2 = np.concatenate([tri, tri], axis=1)
    r = np.arange(SUBLANES * LANES)
    sel = (((np.arange(LANES)[None, :] + (r // LANES)[:, None]) % SUBLANES) == 0).astype(np.float32)
    return jnp.asarray(tri2, BF16), jnp.asarray(sel, BF16)


def _hgrn2(hq, hf, hi, hg, lbl, nw, batch, seq):
    n = hq.shape[0]
    ns = seq // HG_STEP
    tri2, sel = _hgrn2_consts()
    row = pl.BlockSpec((HG_STEP, HG_WIDTH), lambda b, c: (b * ns + c, 0))
    const = lambda s: pl.BlockSpec(s, lambda b, c: (0, 0), pipeline_mode=pl.Buffered(1))
    return pl.pallas_call(
        _hgrn2_kernel,
        grid=(batch, ns),
        in_specs=[row, row, row, row, const(lbl.shape), const((1, HG_DV)),
                  const(tri2.shape), const(sel.shape)],
        out_specs=row,
        out_shape=jax.ShapeDtypeStruct((n, HG_WIDTH), BF16),
        scratch_shapes=[pltpu.VMEM((HG_HEADS, HG_DV, HG_DK), F32)],
        compiler_params=pltpu.CompilerParams(
            dimension_semantics=("arbitrary", "arbitrary"), vmem_limit_bytes=VMEM_LIMIT),
        name="hgrn2",
    )(hq, hf, hi, hg, lbl, nw, tri2, sel)


def _attn_kernel(sink_ref, bias_ref, aq_ref, akp_ref, akc_ref, avp_ref, avc_ref, o_ref):
    T = ATT_BLOCK
    G = ATT_GROUP
    first = pl.program_id(1) == 0
    k_all = jnp.concatenate([akp_ref[...], akc_ref[...]], axis=0)
    v_all = jnp.concatenate([avp_ref[...], avc_ref[...]], axis=0)
    vt_all = v_all.astype(F32).T.astype(BF16)
    lane = lax.broadcasted_iota(jnp.int32, (2 * T, LANES), 1)
    lo = (lane < ATT_HD).astype(F32).astype(BF16)
    hi = (lane >= ATT_HD).astype(F32).astype(BF16)
    rowi = lax.broadcasted_iota(jnp.int32, (LANES, 2 * T), 0)
    lo_r = (rowi < ATT_HD).astype(F32).astype(BF16)
    hi_r = (rowi >= ATT_HD).astype(F32).astype(BF16)
    lo_out = lax.broadcasted_iota(jnp.int32, (LANES, G * T), 0) < ATT_HD
    sink = [jnp.concatenate([jnp.full((1, T), sink_ref[g * G + c], F32) for c in range(G)], axis=1)
            for g in range(ATT_KV_HEADS)]
    for j in range(ATT_TQ // T):
        k = k_all[j * T:(j + 2) * T, :]
        vt = vt_all[:, j * T:(j + 2) * T]
        kcat = jnp.concatenate([k * lo, k * hi], axis=0)
        vcat_t = jnp.concatenate([vt * lo_r, vt * hi_r], axis=1)
        qs = jnp.concatenate([aq_ref[j * T:(j + 1) * T, c * LANES:(c + 1) * LANES]
                              for c in range(G)], axis=0)
        bias = bias_ref[jnp.where(first, 1, 0)] if j == 0 else bias_ref[0]
        bias = jnp.concatenate([bias] * G, axis=1)
        st = _dot_nt(kcat, qs)
        ps, dens = [], []
        for g in range(ATT_KV_HEADS):
            sg = st[g * 2 * T:(g + 1) * 2 * T, :] + bias
            m = jnp.maximum(jnp.max(sg, axis=0, keepdims=True), sink[g])
            p = jnp.exp(sg - m)
            dens.append(jnp.sum(p, axis=0, keepdims=True) + jnp.exp(sink[g] - m))
            ps.append(p.astype(BF16))
        od = _dot(vcat_t, jnp.concatenate(ps, axis=0))
        out = od / jnp.where(lo_out, dens[0], dens[1])
        for c in range(G):
            o_ref[c * LANES:(c + 1) * LANES, j * T:(j + 1) * T] = out[:, c * T:(c + 1) * T].astype(BF16)


def _attn_bias():
    T = ATT_BLOCK
    dist = np.arange(T)[None, :] + T - np.arange(2 * T)[:, None]
    band = (dist >= 0) & (dist < WINDOW)
    first = band & (np.arange(2 * T)[:, None] >= T)
    to_bias = lambda m: np.where(m, 0.0, -np.inf).astype(np.float32)
    return jnp.asarray(np.stack([to_bias(band), to_bias(first)]))


def _attention(sinks, aq, ak, av, batch, seq):
    n = aq.shape[0]
    T = ATT_BLOCK
    TQ = ATT_TQ
    nb = seq // TQ
    r = TQ // T
    bias = _attn_bias()
    cur = lambda w: pl.BlockSpec((TQ, w), lambda b, i: (b * nb + i, 0))
    prev = lambda w: pl.BlockSpec((T, w), lambda b, i: ((b * nb + i) * r - jnp.minimum(i, 1), 0))
    return pl.pallas_call(
        _attn_kernel,
        grid=(batch, nb),
        in_specs=[pl.BlockSpec(memory_space=pltpu.SMEM),
                  pl.BlockSpec(bias.shape, lambda b, i: (0, 0, 0), pipeline_mode=pl.Buffered(1)),
                  cur(ATT_Q_WIDTH), prev(ATT_KV_WIDTH), cur(ATT_KV_WIDTH),
                  prev(ATT_KV_WIDTH), cur(ATT_KV_WIDTH)],
        out_specs=pl.BlockSpec((ATT_Q_WIDTH, TQ), lambda b, i: (0, b * nb + i)),
        out_shape=jax.ShapeDtypeStruct((ATT_Q_WIDTH, n), BF16),
        compiler_params=pltpu.CompilerParams(
            dimension_semantics=("arbitrary", "arbitrary"), vmem_limit_bytes=VMEM_LIMIT),
        name="swa",
    )(sinks, bias, aq, ak, ak, av, av)


FF_CHUNKS = ((0, 1024), (1024, 1024), (2048, 768))


def _ffn_kernel(x_ref, ohg_ref, oatt_t_ref, wo1_ref, wo2_ref, nfw_ref, wg_ref, wu_ref,
                cw_ref, cb_ref, wd_ref, fnw_ref, o_ref, tail_ref):
    tm = x_ref.shape[0]

    @pl.when(pl.program_id(1) == 0)
    def _():
        tail_ref[...] = jnp.zeros_like(tail_ref)

    h = x_ref[...] + _dot(ohg_ref[...], wo1_ref[...]) + _dot_tn(oatt_t_ref[...], wo2_ref[...])
    inv = lax.rsqrt(jnp.mean(h * h, axis=-1, keepdims=True) + EPS)
    v = (h * inv * nfw_ref[...]).astype(BF16)
    y = jnp.zeros((tm, D_MODEL), F32)
    for (c0, cw) in FF_CHUNKS:
        cs = slice(c0, c0 + cw)
        gp = _dot(v, wg_ref[:, cs])
        up = _dot(v, wu_ref[:, cs])
        gx = jnp.concatenate([tail_ref[:, cs], gp], axis=0)
        tail_ref[:, cs] = gp[tm - SUBLANES:, :]
        gate = (cw_ref[0:1, cs] * gx[SUBLANES - 2:SUBLANES - 2 + tm, :]
                + cw_ref[1:2, cs] * gx[SUBLANES - 1:SUBLANES - 1 + tm, :]
                + cw_ref[2:3, cs] * gp) + cb_ref[:, cs]
        act = (gate * _sigmoid(gate) * up).astype(BF16)
        y = y + _dot(act, wd_ref[cs, :])
    h2 = h + y
    inv2 = lax.rsqrt(jnp.mean(h2 * h2, axis=-1, keepdims=True) + EPS)
    o_ref[...] = h2 * inv2 * fnw_ref[...]


def _ffn(x2, ohg, oatt_t, wo1, wo2, nfw, wg, wu, cw, cb, wd, fnw, batch, seq, tm):
    n = x2.shape[0]
    nt = seq // tm
    row = lambda w: pl.BlockSpec((tm, w), lambda b, i: (b * nt + i, 0))
    const = lambda s: pl.BlockSpec(s, lambda b, i: (0, 0), pipeline_mode=pl.Buffered(1))
    return pl.pallas_call(
        _ffn_kernel,
        grid=(batch, nt),
        in_specs=[row(D_MODEL), row(HG_WIDTH),
                  pl.BlockSpec((ATT_Q_WIDTH, tm), lambda b, i: (0, b * nt + i)),
                  const(wo1.shape), const(wo2.shape), const(nfw.shape),
                  const(wg.shape), const(wu.shape), const(cw.shape), const(cb.shape),
                  const(wd.shape), const(fnw.shape)],
        out_specs=row(D_MODEL),
        out_shape=jax.ShapeDtypeStruct((n, D_MODEL), F32),
        scratch_shapes=[pltpu.VMEM((SUBLANES, D_FF), F32)],
        compiler_params=pltpu.CompilerParams(
            dimension_semantics=("arbitrary", "arbitrary"), vmem_limit_bytes=VMEM_LIMIT),
        name="ffn",
    )(x2, ohg, oatt_t, wo1, wo2, nfw, wg, wu, cw, cb, wd, fnw)


def _head_perm():
    idx = []
    for c in range(ATT_GROUP):
        idx += list(range(c * ATT_HD, (c + 1) * ATT_HD))
        idx += list(range((ATT_GROUP + c) * ATT_HD, (ATT_GROUP + c + 1) * ATT_HD))
    return jnp.asarray(idx, jnp.int32)


def kernel(x, norm_mix_w, w_in, b_attn, lb_logits, hg_norm_w, sinks, w_out, norm_ffn_w,
           w_gate, w_up, conv_w, conv_b, w_down, final_norm_w):
    batch, seq, _ = x.shape
    n = batch * seq
    x2 = x.reshape(n, D_MODEL)
    perm = _head_perm()

    w_in0 = w_in[0]
    whg = w_in0[:, :HG_COLS].astype(BF16)
    wq = w_in0[:, HG_COLS:HG_COLS + ATT_Q_WIDTH][:, perm]
    watt = jnp.concatenate([wq, w_in0[:, HG_COLS + ATT_Q_WIDTH:]], axis=1).astype(BF16)
    ba = b_attn[0]
    batt = jnp.concatenate([ba[:ATT_Q_WIDTH][perm], ba[ATT_Q_WIDTH:]])[None, :]
    wo = w_out[0]
    wo1 = wo[:HG_WIDTH].astype(BF16)
    wo2 = wo[HG_WIDTH:][perm].astype(BF16)

    hq, hf, hi, hg, aq, ak, av = _inproj(x2, norm_mix_w[0][None, :], whg, watt, batt, tm=512)
    ohg = _hgrn2(hq, hf, hi, hg, lb_logits, hg_norm_w[0][None, :], batch, seq)
    oatt_t = _attention(sinks[0], aq, ak, av, batch, seq)
    out = _ffn(x2, ohg, oatt_t, wo1, wo2, norm_ffn_w[0][None, :],
               w_gate[0].astype(BF16), w_up[0].astype(BF16), conv_w[0], conv_b[0][None, :],
               w_down[0].astype(BF16), final_norm_w[None, :], batch, seq, tm=512)
    return out.reshape(batch, seq, D_MODEL)
```

```python
import functools
import math

import jax
import jax.numpy as jnp
import numpy as np
from jax import lax
from jax.experimental import pallas as pl
from jax.experimental.pallas import tpu as pltpu

F32 = jnp.float32
BF16 = jnp.bfloat16

D_MODEL = 1024
HG_HEADS = 4
HG_DK = 128
HG_DV = 128
HG_WIDTH = HG_HEADS * HG_DK
ATT_HEADS = 8
ATT_KV_HEADS = 2
ATT_GROUP = ATT_HEADS // ATT_KV_HEADS
ATT_HD = 64
ATT_Q_WIDTH = ATT_HEADS * ATT_HD
ATT_KV_WIDTH = ATT_KV_HEADS * ATT_HD
ATT_BLOCK = 128
WINDOW = 128
HG_COLS = 4 * HG_WIDTH
ATT_COLS = ATT_Q_WIDTH + 2 * ATT_KV_WIDTH
D_FF = 2816
CONV_W = 3
EPS = 1e-6
LOG2E = 1.4426950408889634

LANES = 128
SUBLANES = 8
TILE = 512
HG_CHUNK = 128
HG_NV = HG_CHUNK // SUBLANES
HG_LEVELS = (8, 4, 2, 1)
PIECE = 256
HG_STAGES_PER_PIECE = 3
HG_STAGES_PER_ATTN = 8
VMEM_LIMIT = 56 * 1024 * 1024

PF_COLS = 3 * HG_WIDTH
PB_AQ = HG_WIDTH
PB_AK = PB_AQ + ATT_Q_WIDTH
PB_AV = PB_AK + ATT_KV_WIDTH
PB_COLS = PB_AV + ATT_KV_WIDTH


def _dot(a, b):
    return jnp.dot(a, b, preferred_element_type=F32)


def _dot_nt(a, b):
    return lax.dot_general(a, b, (((1,), (1,)), ((), ())), preferred_element_type=F32)


def _dot_tn(a, b):
    return lax.dot_general(a, b, (((0,), (0,)), ((), ())), preferred_element_type=F32)


def _sigmoid(x):
    return 1.0 / (1.0 + jnp.exp2(x * (-LOG2E)))


def _inproj_pieces(x_ref, nw_ref, whg_ref, watt_ref, batt_ref, pf_ref, pb_ref):
    x = x_ref[...]
    inv = lax.rsqrt(jnp.mean(x * x, axis=-1, keepdims=True) + EPS)
    u = (x * inv * nw_ref[...]).astype(BF16)
    yield
    for j in range(HG_COLS // PIECE):
        c0 = j * PIECE
        p = _dot(u, whg_ref[:, c0:c0 + PIECE])
        if c0 < 2 * HG_WIDTH:
            pf_ref[:, c0:c0 + PIECE] = p
        elif c0 < 3 * HG_WIDTH:
            pb_ref[:, c0 - 2 * HG_WIDTH:c0 - 2 * HG_WIDTH + PIECE] = p.astype(BF16)
        else:
            pf_ref[:, c0 - HG_WIDTH:c0 - HG_WIDTH + PIECE] = p
        yield
    for j in range(ATT_COLS // PIECE):
        c0 = j * PIECE
        a = _dot(u, watt_ref[:, c0:c0 + PIECE]) + batt_ref[:, c0:c0 + PIECE]
        if c0 < ATT_Q_WIDTH:
            a = a * (1.0 / math.sqrt(ATT_HD))
        pb_ref[:, PB_AQ + c0:PB_AQ + c0 + PIECE] = a.astype(BF16)
        yield


def _rows(a):
    return [a[m * SUBLANES:(m + 1) * SUBLANES, :] for m in range(HG_NV)]


def _hgrn2_stages(pf_ref, pb_ref, lbl_ref, nw_ref, tri_ref, o_ref, st_ref, first):
    C = HG_CHUNK
    nch = TILE // C

    l = lbl_ref[...]
    e = jnp.exp(l - jnp.max(l, axis=0, keepdims=True))
    lb = e[0:1, :] / jnp.sum(e, axis=0, keepdims=True)

    f_all = lb + (1.0 - lb) * _sigmoid(pf_ref[:, HG_WIDTH:2 * HG_WIDTH])
    k_all = 1.0 - f_all
    g = jnp.log(f_all)
    g_hi = g.astype(BF16)
    g_lo = (g - g_hi.astype(F32)).astype(BF16)
    q_all = pf_ref[:, 0:HG_WIDTH]

    row = lax.broadcasted_iota(jnp.int32, (C, C), 0)
    col = lax.broadcasted_iota(jnp.int32, (C, C), 1)
    lag_id = jnp.where((row // SUBLANES) == (col // SUBLANES), row - col, -1).reshape(HG_NV, SUBLANES, C)
    half = C // 2
    urow = lax.broadcasted_iota(jnp.int32, (half, C), 0)
    ucol = lax.broadcasted_iota(jnp.int32, (half, C), 1)
    zero8 = jnp.zeros((SUBLANES, LANES), F32)
    zero_blk = jnp.zeros((C, LANES), BF16)
    cat = lambda xs, idx: jnp.concatenate([xs[m] for m in idx], axis=0)

    b_ch, scores_off = {}, {}
    nw = nw_ref[...]

    def pairs(ch):
        rs_ = slice(ch * C, (ch + 1) * C)
        b_all = _dot(tri_ref[...], jnp.concatenate([g_hi[rs_], g_lo[rs_]], axis=0)) * LOG2E
        b_ch[ch] = b_all
        scores_off[ch] = []
        for hp in range(0, HG_HEADS, 2):
            hsl = [slice(h * HG_DK, (h + 1) * HG_DK) for h in (hp, hp + 1)]
            bvs = [_rows(b_all[:, sl]) for sl in hsl]
            qvs = [_rows(q_all[rs_, sl]) for sl in hsl]
            kvs = [_rows(k_all[rs_, sl]) for sl in hsl]
            evs = [[jnp.broadcast_to(x[SUBLANES - 1:SUBLANES, :], (SUBLANES, LANES)) for x in bv]
                   for bv in bvs]
            srows = [[None] * HG_NV for _ in hsl]
            for lv in HG_LEVELS:
                nb = HG_NV // (2 * lv)
                lower = [i * 2 * lv + j for i in range(nb) for j in range(lv)]
                upper = [m + lv for m in lower]
                qhs, kfs = [], []
                for e_ in range(2):
                    r = jnp.concatenate([evs[e_][i * 2 * lv + lv - 1] for i in range(nb) for _ in range(lv)],
                                        axis=0)
                    kh = cat(kvs[e_], lower) * jnp.exp2(r - cat(bvs[e_], lower))
                    qhs.append((cat(qvs[e_], upper) * jnp.exp2(cat(bvs[e_], upper) - r)).astype(BF16))
                    khv = [kh[j * SUBLANES:(j + 1) * SUBLANES, :] for j in range(half // SUBLANES)]
                    kfull = [zero8] * HG_NV
                    for j, m in enumerate(lower):
                        kfull[m] = khv[j]
                    kfs.append(jnp.concatenate(kfull, axis=0).astype(BF16))
                rhs = jnp.concatenate([jnp.concatenate([kfs[0], zero_blk], axis=1),
                                       jnp.concatenate([zero_blk, kfs[1]], axis=1)], axis=0)
                s2 = _dot_nt(jnp.concatenate(qhs, axis=1), rhs)
                for e_ in range(2):
                    s = s2[:, e_ * C:(e_ + 1) * C]
                    if nb > 1:
                        blk = lv * SUBLANES
                        s = jnp.where((urow // blk) == (ucol // (2 * blk)), s, 0.0)
                    for j, m in enumerate(upper):
                        piece = s[j * SUBLANES:(j + 1) * SUBLANES, :]
                        srows[e_][m] = piece if srows[e_][m] is None else srows[e_][m] + piece
            for e_, sl in enumerate(hsl):
                srows[e_][0] = jnp.zeros((SUBLANES, C), F32)
                sc3 = jnp.concatenate(srows[e_], axis=0).reshape(HG_NV, SUBLANES, C)
                f3 = f_all[rs_, sl].reshape(HG_NV, SUBLANES, LANES)
                bd = k_all[rs_, sl].reshape(HG_NV, SUBLANES, LANES)
                q3 = q_all[rs_, sl].reshape(HG_NV, SUBLANES, LANES)
                for d in range(SUBLANES):
                    if d > 0:
                        bd = f3 * pltpu.roll(bd, 1, axis=1)
                    p = jnp.sum(q3 * bd, axis=-1, keepdims=True)
                    sc3 = jnp.where(lag_id == d, p, sc3)
                scores_off[ch].append(sc3.reshape(C, C))
                yield

    def outputs(ch):
        rs_ = slice(ch * C, (ch + 1) * C)
        for h in range(HG_HEADS):
            sl = slice(h * HG_DK, (h + 1) * HG_DK)
            b, q, kk = b_ch[ch][:, sl], q_all[rs_, sl], k_all[rs_, sl]
            vt = pb_ref[rs_, sl].astype(F32).T.astype(BF16)
            scores = scores_off[ch][h]
            st = st_ref[h]
            if ch == 0:
                st = jnp.where(first, 0.0, st)
            lhs = jnp.concatenate([scores.astype(BF16), (q * jnp.exp2(b)).astype(BF16)], axis=1)
            o = _dot_nt(lhs, jnp.concatenate([vt, st.astype(BF16)], axis=1))
            bl = b[C - 1:C, :]
            kt = (kk * jnp.exp2(bl - b)).astype(BF16)
            st_ref[h] = st * jnp.exp2(bl) + _dot(vt, kt)
            inv = lax.rsqrt(jnp.mean(o * o, axis=-1, keepdims=True) + EPS * HG_DK)
            hg = pf_ref[rs_, 2 * HG_WIDTH + h * HG_DK:2 * HG_WIDTH + (h + 1) * HG_DK]
            o_ref[rs_, sl] = (o * inv * nw * (hg * _sigmoid(hg))).astype(BF16)
            yield

    for ch in range(nch + 1):
        if ch < nch:
            yield from pairs(ch)
        if ch >= 1:
            yield from outputs(ch - 1)


def _cumsum_matrix():
    t = np.arange(HG_CHUNK)
    tri = (t[None, :] <= t[:, None]).astype(np.float32)
    return jnp.asarray(np.concatenate([tri, tri], axis=1), BF16)


def _attn_stages(sink_ref, bias_ref, pb_ref, kv_ref, o_ref, first):
    T = ATT_BLOCK
    G = ATT_GROUP
    ak = pb_ref[:, PB_AK:PB_AV]
    av = pb_ref[:, PB_AV:]
    k_all = jnp.concatenate([kv_ref[0], ak], axis=0)
    v_all = jnp.concatenate([kv_ref[1], av], axis=0)
    kv_ref[0] = ak[TILE - T:, :]
    kv_ref[1] = av[TILE - T:, :]
    vt_all = v_all.astype(F32).T.astype(BF16)
    lane = lax.broadcasted_iota(jnp.int32, (2 * T, LANES), 1)
    lo = (lane < ATT_HD).astype(F32).astype(BF16)
    hi = (lane >= ATT_HD).astype(F32).astype(BF16)
    rowi = lax.broadcasted_iota(jnp.int32, (LANES, 2 * T), 0)
    lo_r = (rowi < ATT_HD).astype(F32).astype(BF16)
    hi_r = (rowi >= ATT_HD).astype(F32).astype(BF16)
    lo_out = lax.broadcasted_iota(jnp.int32, (LANES, G * T), 0) < ATT_HD
    sink = [jnp.concatenate([jnp.full((1, T), sink_ref[g * G + c], F32) for c in range(G)], axis=1)
            for g in range(ATT_KV_HEADS)]
    for j in range(TILE // T):
        k = k_all[j * T:(j + 2) * T, :]
        vt = vt_all[:, j * T:(j + 2) * T]
        kcat = jnp.concatenate([k * lo, k * hi], axis=0)
        vcat_t = jnp.concatenate([vt * lo_r, vt * hi_r], axis=1)
        qs = jnp.concatenate([pb_ref[j * T:(j + 1) * T, PB_AQ + c * LANES:PB_AQ + (c + 1) * LANES]
                              for c in range(G)], axis=0)
        bias = bias_ref[jnp.where(first, 1, 0)] if j == 0 else bias_ref[0]
        bias = jnp.concatenate([bias] * G, axis=1)
        st = _dot_nt(kcat, qs)
        ps, dens = [], []
        for g in range(ATT_KV_HEADS):
            sg = st[g * 2 * T:(g + 1) * 2 * T, :] + bias
            m = jnp.maximum(jnp.max(sg, axis=0, keepdims=True), sink[g])
            p = jnp.exp(sg - m)
            dens.append(jnp.sum(p, axis=0, keepdims=True) + jnp.exp(sink[g] - m))
            ps.append(p.astype(BF16))
        od = _dot(vcat_t, jnp.concatenate(ps, axis=0))
        out = od / jnp.where(lo_out, dens[0], dens[1])
        for c in range(G):
            o_ref[c * LANES:(c + 1) * LANES, j * T:(j + 1) * T] = out[:, c * T:(c + 1) * T].astype(BF16)
        yield


def _attn_bias():
    T = ATT_BLOCK
    dist = np.arange(T)[None, :] + T - np.arange(2 * T)[:, None]
    band = (dist >= 0) & (dist < WINDOW)
    first = band & (np.arange(2 * T)[:, None] >= T)
    to_bias = lambda m: np.where(m, 0.0, -np.inf).astype(np.float32)
    return jnp.asarray(np.stack([to_bias(band), to_bias(first)]))


def _mixer_kernel(tiles_per_seq, sink_ref, x_ref, nw_ref, whg_ref, watt_ref, batt_ref, lbl_ref,
                  hnw_ref, tri_ref, bias_ref, ohg_ref, oatt_ref,
                  pf0_ref, pb0_ref, pf1_ref, pb1_ref, st_ref, kv_ref):
    t = pl.program_id(0)

    @pl.when(t == 0)
    def _():
        pf1_ref[...] = jnp.zeros_like(pf1_ref)
        pb1_ref[...] = jnp.zeros_like(pb1_ref)
        st_ref[...] = jnp.zeros_like(st_ref)
        kv_ref[...] = jnp.zeros_like(kv_ref)

    first = ((t + tiles_per_seq - 1) % tiles_per_seq) == 0

    def step(pf_in, pb_in, pf_out, pb_out):
        pieces = _inproj_pieces(x_ref, nw_ref, whg_ref, watt_ref, batt_ref, pf_out, pb_out)
        hgrn2 = _hgrn2_stages(pf_in, pb_in, lbl_ref, hnw_ref, tri_ref, ohg_ref, st_ref, first)
        attn = _attn_stages(sink_ref, bias_ref, pb_in, kv_ref, oatt_ref, first)
        next(pieces)
        for i, _ in enumerate(hgrn2):
            if i % HG_STAGES_PER_PIECE == 0:
                next(pieces, None)
            if i % HG_STAGES_PER_ATTN == HG_STAGES_PER_ATTN - 1:
                next(attn, None)
        for _ in attn:
            pass
        for _ in pieces:
            pass

    @pl.when(t % 2 == 0)
    def _():
        step(pf1_ref, pb1_ref, pf0_ref, pb0_ref)

    @pl.when(t % 2 == 1)
    def _():
        step(pf0_ref, pb0_ref, pf1_ref, pb1_ref)


def _mixer(sinks, x2, nw, whg, watt, batt, lbl, hnw, seq):
    n = x2.shape[0]
    nt = n // TILE
    tri2 = _cumsum_matrix()
    bias = _attn_bias()
    const = lambda a: pl.BlockSpec(a.shape, lambda t: (0,) * a.ndim, pipeline_mode=pl.Buffered(1))
    return pl.pallas_call(
        functools.partial(_mixer_kernel, seq // TILE),
        grid=(nt + 1,),
        in_specs=[pl.BlockSpec(memory_space=pltpu.SMEM),
                  pl.BlockSpec((TILE, D_MODEL), lambda t: (jnp.minimum(t, nt - 1), 0)),
                  const(nw), const(whg), const(watt), const(batt), const(lbl), const(hnw),
                  const(tri2), const(bias)],
        out_specs=[pl.BlockSpec((TILE, HG_WIDTH), lambda t: (jnp.maximum(t - 1, 0), 0)),
                   pl.BlockSpec((ATT_Q_WIDTH, TILE), lambda t: (0, jnp.maximum(t - 1, 0)))],
        out_shape=[jax.ShapeDtypeStruct((n, HG_WIDTH), BF16),
                   jax.ShapeDtypeStruct((ATT_Q_WIDTH, n), BF16)],
        scratch_shapes=[pltpu.VMEM((TILE, PF_COLS), F32), pltpu.VMEM((TILE, PB_COLS), BF16),
                        pltpu.VMEM((TILE, PF_COLS), F32), pltpu.VMEM((TILE, PB_COLS), BF16),
                        pltpu.VMEM((HG_HEADS, HG_DV, HG_DK), F32),
                        pltpu.VMEM((2, ATT_BLOCK, ATT_KV_WIDTH), BF16)],
        compiler_params=pltpu.CompilerParams(
            dimension_semantics=("arbitrary",), vmem_limit_bytes=VMEM_LIMIT),
        name="mixer",
    )(sinks, x2, nw, whg, watt, batt, lbl, hnw, tri2, bias)


FF_CHUNKS = ((0, 1024), (1024, 1024), (2048, 768))


def _ffn_kernel(x_ref, ohg_ref, oatt_t_ref, wo1_ref, wo2_ref, nfw_ref, wg_ref, wu_ref,
                cw_ref, cb_ref, wd_ref, fnw_ref, o_ref, tail_ref):
    tm = x_ref.shape[0]

    @pl.when(pl.program_id(1) == 0)
    def _():
        tail_ref[...] = jnp.zeros_like(tail_ref)

    h = x_ref[...] + _dot(ohg_ref[...], wo1_ref[...]) + _dot_tn(oatt_t_ref[...], wo2_ref[...])
    inv = lax.rsqrt(jnp.mean(h * h, axis=-1, keepdims=True) + EPS)
    v = (h * inv * nfw_ref[...]).astype(BF16)
    y = jnp.zeros((tm, D_MODEL), F32)
    for (c0, cw) in FF_CHUNKS:
        cs = slice(c0, c0 + cw)
        gp = _dot(v, wg_ref[:, cs])
        up = _dot(v, wu_ref[:, cs])
        gx = jnp.concatenate([tail_ref[:, cs], gp], axis=0)
        tail_ref[:, cs] = gp[tm - SUBLANES:, :]
        gate = (cw_ref[0:1, cs] * gx[SUBLANES - 2:SUBLANES - 2 + tm, :]
                + cw_ref[1:2, cs] * gx[SUBLANES - 1:SUBLANES - 1 + tm, :]
                + cw_ref[2:3, cs] * gp) + cb_ref[:, cs]
        act = (gate * _sigmoid(gate) * up).astype(BF16)
        y = y + _dot(act, wd_ref[cs, :])
    h2 = h + y
    inv2 = lax.rsqrt(jnp.mean(h2 * h2, axis=-1, keepdims=True) + EPS)
    o_ref[...] = h2 * inv2 * fnw_ref[...]


def _ffn(x2, ohg, oatt_t, wo1, wo2, nfw, wg, wu, cw, cb, wd, fnw, batch, seq):
    n = x2.shape[0]
    tm = TILE
    nt = seq // tm
    row = lambda w: pl.BlockSpec((tm, w), lambda b, i: (b * nt + i, 0))
    const = lambda s: pl.BlockSpec(s, lambda b, i: (0, 0), pipeline_mode=pl.Buffered(1))
    return pl.pallas_call(
        _ffn_kernel,
        grid=(batch, nt),
        in_specs=[row(D_MODEL), row(HG_WIDTH),
                  pl.BlockSpec((ATT_Q_WIDTH, tm), lambda b, i: (0, b * nt + i)),
                  const(wo1.shape), const(wo2.shape), const(nfw.shape),
                  const(wg.shape), const(wu.shape), const(cw.shape), const(cb.shape),
                  const(wd.shape), const(fnw.shape)],
        out_specs=row(D_MODEL),
        out_shape=jax.ShapeDtypeStruct((n, D_MODEL), F32),
        scratch_shapes=[pltpu.VMEM((SUBLANES, D_FF), F32)],
        compiler_params=pltpu.CompilerParams(
            dimension_semantics=("arbitrary", "arbitrary"), vmem_limit_bytes=VMEM_LIMIT),
        name="ffn",
    )(x2, ohg, oatt_t, wo1, wo2, nfw, wg, wu, cw, cb, wd, fnw)


def _head_perm():
    idx = []
    for c in range(ATT_GROUP):
        idx += list(range(c * ATT_HD, (c + 1) * ATT_HD))
        idx += list(range((ATT_GROUP + c) * ATT_HD, (ATT_GROUP + c + 1) * ATT_HD))
    return jnp.asarray(idx, jnp.int32)


def kernel(x, norm_mix_w, w_in, b_attn, lb_logits, hg_norm_w, sinks, w_out, norm_ffn_w,
           w_gate, w_up, conv_w, conv_b, w_down, final_norm_w):
    batch, seq, _ = x.shape
    n = batch * seq
    assert seq % TILE == 0
    x2 = x.reshape(n, D_MODEL)
    perm = _head_perm()

    w_in0 = w_in[0]
    whg = w_in0[:, :HG_COLS].astype(BF16)
    wq = w_in0[:, HG_COLS:HG_COLS + ATT_Q_WIDTH][:, perm]
    watt = jnp.concatenate([wq, w_in0[:, HG_COLS + ATT_Q_WIDTH:]], axis=1).astype(BF16)
    ba = b_attn[0]
    batt = jnp.concatenate([ba[:ATT_Q_WIDTH][perm], ba[ATT_Q_WIDTH:]])[None, :]
    wo = w_out[0]
    wo1 = wo[:HG_WIDTH].astype(BF16)
    wo2 = wo[HG_WIDTH:][perm].astype(BF16)

    ohg, oatt_t = _mixer(sinks[0], x2, norm_mix_w[0][None, :], whg, watt, batt, lb_logits,
                         hg_norm_w[0][None, :], seq)
    out = _ffn(x2, ohg, oatt_t, wo1, wo2, norm_ffn_w[0][None, :],
               w_gate[0].astype(BF16), w_up[0].astype(BF16), conv_w[0], conv_b[0][None, :],
               w_down[0].astype(BF16), final_norm_w[None, :], batch, seq)
    return out.reshape(batch, seq, D_MODEL)
```

```python
import functools
import math

import jax
import jax.numpy as jnp
import numpy as np
from jax import lax
from jax.experimental import pallas as pl
from jax.experimental.pallas import tpu as pltpu

F32 = jnp.float32
BF16 = jnp.bfloat16

D_MODEL = 1024
HG_HEADS = 4
HG_DK = 128
HG_DV = 128
HG_WIDTH = HG_HEADS * HG_DK
ATT_HEADS = 8
ATT_KV_HEADS = 2
ATT_GROUP = ATT_HEADS // ATT_KV_HEADS
ATT_HD = 64
ATT_Q_WIDTH = ATT_HEADS * ATT_HD
ATT_KV_WIDTH = ATT_KV_HEADS * ATT_HD
ATT_BLOCK = 128
WINDOW = 128
HG_COLS = 4 * HG_WIDTH
ATT_COLS = ATT_Q_WIDTH + 2 * ATT_KV_WIDTH
D_FF = 2816
CONV_W = 3
EPS = 1e-6
LOG2E = 1.4426950408889634

LANES = 128
SUBLANES = 8
TILE = 512
HG_CHUNK = 128
HG_NV = HG_CHUNK // SUBLANES
HG_LEVELS = (8, 4, 2, 1)
PIECE = 256
HG_OUTPUT_LAG = 1
HG_STAGES_PER_PIECE = 3
HG_STAGES_PER_ATTN = 8
VMEM_LIMIT = 56 * 1024 * 1024

PF_COLS = 3 * HG_WIDTH
PB_AQ = HG_WIDTH
PB_AK = PB_AQ + ATT_Q_WIDTH
PB_AV = PB_AK + ATT_KV_WIDTH
PB_COLS = PB_AV + ATT_KV_WIDTH


def _dot(a, b):
    return jnp.dot(a, b, preferred_element_type=F32)


def _dot_nt(a, b):
    return lax.dot_general(a, b, (((1,), (1,)), ((), ())), preferred_element_type=F32)


def _dot_tn(a, b):
    return lax.dot_general(a, b, (((0,), (0,)), ((), ())), preferred_element_type=F32)


def _sigmoid(x):
    return 1.0 / (1.0 + jnp.exp2(x * (-LOG2E)))


def _inproj_pieces(x_ref, nw_ref, whg_ref, watt_ref, batt_ref, pf_ref, pb_ref):
    x = x_ref[...]
    inv = lax.rsqrt(jnp.mean(x * x, axis=-1, keepdims=True) + EPS)
    u = (x * inv * nw_ref[...]).astype(BF16)
    yield
    for j in range(HG_COLS // PIECE):
        c0 = j * PIECE
        p = _dot(u, whg_ref[:, c0:c0 + PIECE])
        if c0 < 2 * HG_WIDTH:
            pf_ref[:, c0:c0 + PIECE] = p
        elif c0 < 3 * HG_WIDTH:
            pb_ref[:, c0 - 2 * HG_WIDTH:c0 - 2 * HG_WIDTH + PIECE] = p.astype(BF16)
        else:
            pf_ref[:, c0 - HG_WIDTH:c0 - HG_WIDTH + PIECE] = p
        yield
    for j in range(ATT_COLS // PIECE):
        c0 = j * PIECE
        a = _dot(u, watt_ref[:, c0:c0 + PIECE]) + batt_ref[:, c0:c0 + PIECE]
        if c0 < ATT_Q_WIDTH:
            a = a * (1.0 / math.sqrt(ATT_HD))
        pb_ref[:, PB_AQ + c0:PB_AQ + c0 + PIECE] = a.astype(BF16)
        yield


def _rows(a):
    return [a[m * SUBLANES:(m + 1) * SUBLANES, :] for m in range(HG_NV)]


def _hgrn2_stages(pf_ref, pb_ref, lbl_ref, nw_ref, tri_ref, o_ref, st_ref, first):
    C = HG_CHUNK
    nch = TILE // C

    l = lbl_ref[...]
    e = jnp.exp(l - jnp.max(l, axis=0, keepdims=True))
    lb = e[0:1, :] / jnp.sum(e, axis=0, keepdims=True)

    f_all = lb + (1.0 - lb) * _sigmoid(pf_ref[:, HG_WIDTH:2 * HG_WIDTH])
    k_all = 1.0 - f_all
    g = jnp.log(f_all)
    g_hi = g.astype(BF16)
    g_lo = (g - g_hi.astype(F32)).astype(BF16)
    q_all = pf_ref[:, 0:HG_WIDTH]

    row = lax.broadcasted_iota(jnp.int32, (C, C), 0)
    col = lax.broadcasted_iota(jnp.int32, (C, C), 1)
    lag_id = jnp.where((row // SUBLANES) == (col // SUBLANES), row - col, -1).reshape(HG_NV, SUBLANES, C)
    half = C // 2
    urow = lax.broadcasted_iota(jnp.int32, (half, C), 0)
    ucol = lax.broadcasted_iota(jnp.int32, (half, C), 1)
    zero8 = jnp.zeros((SUBLANES, LANES), F32)
    zero_blk = jnp.zeros((C, LANES), BF16)
    cat = lambda xs, idx: jnp.concatenate([xs[m] for m in idx], axis=0)

    b_ch, scores_off = {}, {}
    nw = nw_ref[...]

    def pairs(ch):
        rs_ = slice(ch * C, (ch + 1) * C)
        b_all = _dot(tri_ref[...], jnp.concatenate([g_hi[rs_], g_lo[rs_]], axis=0)) * LOG2E
        b_ch[ch] = b_all
        scores_off[ch] = []
        for hp in range(0, HG_HEADS, 2):
            hsl = [slice(h * HG_DK, (h + 1) * HG_DK) for h in (hp, hp + 1)]
            bvs = [_rows(b_all[:, sl]) for sl in hsl]
            qvs = [_rows(q_all[rs_, sl]) for sl in hsl]
            kvs = [_rows(k_all[rs_, sl]) for sl in hsl]
            evs = [[jnp.broadcast_to(x[SUBLANES - 1:SUBLANES, :], (SUBLANES, LANES)) for x in bv]
                   for bv in bvs]
            srows = [[None] * HG_NV for _ in hsl]
            for lv in HG_LEVELS:
                nb = HG_NV // (2 * lv)
                lower = [i * 2 * lv + j for i in range(nb) for j in range(lv)]
                upper = [m + lv for m in lower]
                qhs, kfs = [], []
                for e_ in range(2):
                    r = jnp.concatenate([evs[e_][i * 2 * lv + lv - 1] for i in range(nb) for _ in range(lv)],
                                        axis=0)
                    kh = cat(kvs[e_], lower) * jnp.exp2(r - cat(bvs[e_], lower))
                    qhs.append((cat(qvs[e_], upper) * jnp.exp2(cat(bvs[e_], upper) - r)).astype(BF16))
                    khv = [kh[j * SUBLANES:(j + 1) * SUBLANES, :] for j in range(half // SUBLANES)]
                    kfull = [zero8] * HG_NV
                    for j, m in enumerate(lower):
                        kfull[m] = khv[j]
                    kfs.append(jnp.concatenate(kfull, axis=0).astype(BF16))
                rhs = jnp.concatenate([jnp.concatenate([kfs[0], zero_blk], axis=1),
                                       jnp.concatenate([zero_blk, kfs[1]], axis=1)], axis=0)
                s2 = _dot_nt(jnp.concatenate(qhs, axis=1), rhs)
                for e_ in range(2):
                    s = s2[:, e_ * C:(e_ + 1) * C]
                    if nb > 1:
                        blk = lv * SUBLANES
                        s = jnp.where((urow // blk) == (ucol // (2 * blk)), s, 0.0)
                    for j, m in enumerate(upper):
                        piece = s[j * SUBLANES:(j + 1) * SUBLANES, :]
                        srows[e_][m] = piece if srows[e_][m] is None else srows[e_][m] + piece
            for e_, sl in enumerate(hsl):
                srows[e_][0] = jnp.zeros((SUBLANES, C), F32)
                sc3 = jnp.concatenate(srows[e_], axis=0).reshape(HG_NV, SUBLANES, C)
                f3 = f_all[rs_, sl].reshape(HG_NV, SUBLANES, LANES)
                bd = k_all[rs_, sl].reshape(HG_NV, SUBLANES, LANES)
                q3 = q_all[rs_, sl].reshape(HG_NV, SUBLANES, LANES)
                for d in range(SUBLANES):
                    if d > 0:
                        bd = f3 * pltpu.roll(bd, 1, axis=1)
                    p = jnp.sum(q3 * bd, axis=-1, keepdims=True)
                    sc3 = jnp.where(lag_id == d, p, sc3)
                scores_off[ch].append(sc3.reshape(C, C))
                yield "vector"

    def outputs(ch):
        rs_ = slice(ch * C, (ch + 1) * C)
        for h in range(HG_HEADS):
            sl = slice(h * HG_DK, (h + 1) * HG_DK)
            b, q, kk = b_ch[ch][:, sl], q_all[rs_, sl], k_all[rs_, sl]
            vt = pb_ref[rs_, sl].astype(F32).T.astype(BF16)
            scores = scores_off[ch][h]
            st = st_ref[h]
            if ch == 0:
                st = jnp.where(first, 0.0, st)
            lhs = jnp.concatenate([scores.astype(BF16), (q * jnp.exp2(b)).astype(BF16)], axis=1)
            o = _dot_nt(lhs, jnp.concatenate([vt, st.astype(BF16)], axis=1))
            bl = b[C - 1:C, :]
            kt = (kk * jnp.exp2(bl - b)).astype(BF16)
            st_ref[h] = st * jnp.exp2(bl) + _dot(vt, kt)
            inv = lax.rsqrt(jnp.mean(o * o, axis=-1, keepdims=True) + EPS * HG_DK)
            hg = pf_ref[rs_, 2 * HG_WIDTH + h * HG_DK:2 * HG_WIDTH + (h + 1) * HG_DK]
            o_ref[rs_, sl] = (o * inv * nw * (hg * _sigmoid(hg))).astype(BF16)
            yield "matrix" if h == 1 else "other"

    for ch in range(nch + HG_OUTPUT_LAG):
        if ch < nch:
            yield from pairs(ch)
        if ch >= HG_OUTPUT_LAG:
            yield from outputs(ch - HG_OUTPUT_LAG)


def _cumsum_matrix():
    t = np.arange(HG_CHUNK)
    tri = (t[None, :] <= t[:, None]).astype(np.float32)
    return jnp.asarray(np.concatenate([tri, tri], axis=1), BF16)


def _attn_stages(sink_ref, bias_ref, pb_ref, kv_ref, o_ref, first):
    T = ATT_BLOCK
    G = ATT_GROUP
    ak = pb_ref[:, PB_AK:PB_AV]
    av = pb_ref[:, PB_AV:]
    k_all = jnp.concatenate([kv_ref[0], ak], axis=0)
    v_all = jnp.concatenate([kv_ref[1], av], axis=0)
    kv_ref[0] = ak[TILE - T:, :]
    kv_ref[1] = av[TILE - T:, :]
    vt_all = v_all.astype(F32).T.astype(BF16)
    lane = lax.broadcasted_iota(jnp.int32, (2 * T, LANES), 1)
    lo = (lane < ATT_HD).astype(F32).astype(BF16)
    hi = (lane >= ATT_HD).astype(F32).astype(BF16)
    rowi = lax.broadcasted_iota(jnp.int32, (LANES, 2 * T), 0)
    lo_r = (rowi < ATT_HD).astype(F32).astype(BF16)
    hi_r = (rowi >= ATT_HD).astype(F32).astype(BF16)
    lo_out = lax.broadcasted_iota(jnp.int32, (LANES, G * T), 0) < ATT_HD
    sink = [jnp.concatenate([jnp.full((1, T), sink_ref[g * G + c], F32) for c in range(G)], axis=1)
            for g in range(ATT_KV_HEADS)]
    for j in range(TILE // T):
        k = k_all[j * T:(j + 2) * T, :]
        vt = vt_all[:, j * T:(j + 2) * T]
        kcat = jnp.concatenate([k * lo, k * hi], axis=0)
        vcat_t = jnp.concatenate([vt * lo_r, vt * hi_r], axis=1)
        qs = jnp.concatenate([pb_ref[j * T:(j + 1) * T, PB_AQ + c * LANES:PB_AQ + (c + 1) * LANES]
                              for c in range(G)], axis=0)
        bias = bias_ref[jnp.where(first, 1, 0)] if j == 0 else bias_ref[0]
        bias = jnp.concatenate([bias] * G, axis=1)
        st = _dot_nt(kcat, qs)
        ps, dens = [], []
        for g in range(ATT_KV_HEADS):
            sg = st[g * 2 * T:(g + 1) * 2 * T, :] + bias
            m = jnp.maximum(jnp.max(sg, axis=0, keepdims=True), sink[g])
            p = jnp.exp(sg - m)
            dens.append(jnp.sum(p, axis=0, keepdims=True) + jnp.exp(sink[g] - m))
            ps.append(p.astype(BF16))
        od = _dot(vcat_t, jnp.concatenate(ps, axis=0))
        out = od / jnp.where(lo_out, dens[0], dens[1])
        for c in range(G):
            o_ref[c * LANES:(c + 1) * LANES, j * T:(j + 1) * T] = out[:, c * T:(c + 1) * T].astype(BF16)
        yield


def _attn_bias():
    T = ATT_BLOCK
    dist = np.arange(T)[None, :] + T - np.arange(2 * T)[:, None]
    band = (dist >= 0) & (dist < WINDOW)
    first = band & (np.arange(2 * T)[:, None] >= T)
    to_bias = lambda m: np.where(m, 0.0, -np.inf).astype(np.float32)
    return jnp.asarray(np.stack([to_bias(band), to_bias(first)]))


def _mixer_kernel(tiles_per_seq, sink_ref, x_ref, nw_ref, whg_ref, watt_ref, batt_ref, lbl_ref,
                  hnw_ref, tri_ref, bias_ref, ohg_ref, oatt_ref,
                  pf0_ref, pb0_ref, pf1_ref, pb1_ref, st_ref, kv_ref):
    t = pl.program_id(0)

    @pl.when(t == 0)
    def _():
        pf1_ref[...] = jnp.zeros_like(pf1_ref)
        pb1_ref[...] = jnp.zeros_like(pb1_ref)
        st_ref[...] = jnp.zeros_like(st_ref)
        kv_ref[...] = jnp.zeros_like(kv_ref)

    first = ((t + tiles_per_seq - 1) % tiles_per_seq) == 0

    def step(pf_in, pb_in, pf_out, pb_out):
        pieces = _inproj_pieces(x_ref, nw_ref, whg_ref, watt_ref, batt_ref, pf_out, pb_out)
        hgrn2 = _hgrn2_stages(pf_in, pb_in, lbl_ref, hnw_ref, tri_ref, ohg_ref, st_ref, first)
        attn = _attn_stages(sink_ref, bias_ref, pb_in, kv_ref, oatt_ref, first)
        next(pieces)
        for i, kind in enumerate(hgrn2):
            if i % HG_STAGES_PER_PIECE == 0:
                next(pieces, None)
            if i % HG_STAGES_PER_ATTN == HG_STAGES_PER_ATTN - 1:
                next(attn, None)
        for _ in attn:
            pass
        for _ in pieces:
            pass

    @pl.when(t % 2 == 0)
    def _():
        step(pf1_ref, pb1_ref, pf0_ref, pb0_ref)

    @pl.when(t % 2 == 1)
    def _():
        step(pf0_ref, pb0_ref, pf1_ref, pb1_ref)


def _mixer(sinks, x2, nw, whg, watt, batt, lbl, hnw, seq):
    n = x2.shape[0]
    nt = n // TILE
    tri2 = _cumsum_matrix()
    bias = _attn_bias()
    const = lambda a: pl.BlockSpec(a.shape, lambda t: (0,) * a.ndim, pipeline_mode=pl.Buffered(1))
    return pl.pallas_call(
        functools.partial(_mixer_kernel, seq // TILE),
        grid=(nt + 1,),
        in_specs=[pl.BlockSpec(memory_space=pltpu.SMEM),
                  pl.BlockSpec((TILE, D_MODEL), lambda t: (jnp.minimum(t, nt - 1), 0)),
                  const(nw), const(whg), const(watt), const(batt), const(lbl), const(hnw),
                  const(tri2), const(bias)],
        out_specs=[pl.BlockSpec((TILE, HG_WIDTH), lambda t: (jnp.maximum(t - 1, 0), 0)),
                   pl.BlockSpec((ATT_Q_WIDTH, TILE), lambda t: (0, jnp.maximum(t - 1, 0)))],
        out_shape=[jax.ShapeDtypeStruct((n, HG_WIDTH), BF16),
                   jax.ShapeDtypeStruct((ATT_Q_WIDTH, n), BF16)],
        scratch_shapes=[pltpu.VMEM((TILE, PF_COLS), F32), pltpu.VMEM((TILE, PB_COLS), BF16),
                        pltpu.VMEM((TILE, PF_COLS), F32), pltpu.VMEM((TILE, PB_COLS), BF16),
                        pltpu.VMEM((HG_HEADS, HG_DV, HG_DK), F32),
                        pltpu.VMEM((2, ATT_BLOCK, ATT_KV_WIDTH), BF16)],
        compiler_params=pltpu.CompilerParams(
            dimension_semantics=("arbitrary",), vmem_limit_bytes=VMEM_LIMIT),
        name="mixer",
    )(sinks, x2, nw, whg, watt, batt, lbl, hnw, tri2, bias)


FF_CHUNKS = ((0, 1024), (1024, 1024), (2048, 768))
FFN_TILES = 2


def _ffn_kernel(x_ref, ohg_ref, oatt_t_ref, wo1_ref, wo2_ref, nfw_ref, wg_ref, wu_ref,
                cw_ref, cb_ref, wd_ref, fnw_ref, o_ref, tail_ref):
    tm = TILE

    @pl.when(pl.program_id(1) == 0)
    def _():
        tail_ref[...] = jnp.zeros_like(tail_ref)

    hs, vs = [], []
    for r in range(FFN_TILES):
        rows = slice(r * tm, (r + 1) * tm)
        h = (x_ref[rows, :] + _dot(ohg_ref[rows, :], wo1_ref[...])
             + _dot_tn(oatt_t_ref[:, rows], wo2_ref[...]))
        inv = lax.rsqrt(jnp.mean(h * h, axis=-1, keepdims=True) + EPS)
        hs.append(h)
        vs.append((h * inv * nfw_ref[...]).astype(BF16))
    tails = [tail_ref[:, c0:c0 + cw] for (c0, cw) in FF_CHUNKS]
    for r in range(FFN_TILES):
        rows = slice(r * tm, (r + 1) * tm)
        y = jnp.zeros((tm, D_MODEL), F32)
        for ci, (c0, cw) in enumerate(FF_CHUNKS):
            cs = slice(c0, c0 + cw)
            gp = _dot(vs[r], wg_ref[:, cs])
            up = _dot(vs[r], wu_ref[:, cs])
            gx = jnp.concatenate([tails[ci], gp], axis=0)
            tails[ci] = gp[tm - SUBLANES:, :]
            gate = (cw_ref[0:1, cs] * gx[SUBLANES - 2:SUBLANES - 2 + tm, :]
                    + cw_ref[1:2, cs] * gx[SUBLANES - 1:SUBLANES - 1 + tm, :]
                    + cw_ref[2:3, cs] * gp) + cb_ref[:, cs]
            act = (gate * _sigmoid(gate) * up).astype(BF16)
            y = y + _dot(act, wd_ref[cs, :])
        h2 = hs[r] + y
        inv2 = lax.rsqrt(jnp.mean(h2 * h2, axis=-1, keepdims=True) + EPS)
        o_ref[rows, :] = h2 * inv2 * fnw_ref[...]
    for ci, (c0, cw) in enumerate(FF_CHUNKS):
        tail_ref[:, c0:c0 + cw] = tails[ci]


def _ffn(x2, ohg, oatt_t, wo1, wo2, nfw, wg, wu, cw, cb, wd, fnw, batch, seq):
    n = x2.shape[0]
    tm = FFN_TILES * TILE
    assert seq % tm == 0
    nt = seq // tm
    row = lambda w: pl.BlockSpec((tm, w), lambda b, i: (b * nt + i, 0))
    const = lambda s: pl.BlockSpec(s, lambda b, i: (0, 0), pipeline_mode=pl.Buffered(1))
    return pl.pallas_call(
        _ffn_kernel,
        grid=(batch, nt),
        in_specs=[row(D_MODEL), row(HG_WIDTH),
                  pl.BlockSpec((ATT_Q_WIDTH, tm), lambda b, i: (0, b * nt + i)),
                  const(wo1.shape), const(wo2.shape), const(nfw.shape),
                  const(wg.shape), const(wu.shape), const(cw.shape), const(cb.shape),
                  const(wd.shape), const(fnw.shape)],
        out_specs=row(D_MODEL),
        out_shape=jax.ShapeDtypeStruct((n, D_MODEL), F32),
        scratch_shapes=[pltpu.VMEM((SUBLANES, D_FF), F32)],
        compiler_params=pltpu.CompilerParams(
            dimension_semantics=("arbitrary", "arbitrary"), vmem_limit_bytes=VMEM_LIMIT),
        name="ffn",
    )(x2, ohg, oatt_t, wo1, wo2, nfw, wg, wu, cw, cb, wd, fnw)


def _pair_heads(a, axis):
    shape = a.shape
    a = a.reshape(shape[:axis] + (ATT_KV_HEADS, ATT_GROUP, ATT_HD) + shape[axis + 1:])
    return jnp.swapaxes(a, axis, axis + 1).reshape(shape)


def kernel(x, norm_mix_w, w_in, b_attn, lb_logits, hg_norm_w, sinks, w_out, norm_ffn_w,
           w_gate, w_up, conv_w, conv_b, w_down, final_norm_w):
    batch, seq, _ = x.shape
    n = batch * seq
    assert seq % TILE == 0
    x2 = x.reshape(n, D_MODEL)

    w_in0 = w_in[0]
    whg = w_in0[:, :HG_COLS].astype(BF16)
    wq = _pair_heads(w_in0[:, HG_COLS:HG_COLS + ATT_Q_WIDTH], 1)
    watt = jnp.concatenate([wq, w_in0[:, HG_COLS + ATT_Q_WIDTH:]], axis=1).astype(BF16)
    ba = b_attn[0]
    batt = jnp.concatenate([_pair_heads(ba[:ATT_Q_WIDTH], 0), ba[ATT_Q_WIDTH:]])[None, :]
    wo = w_out[0]
    wo1 = wo[:HG_WIDTH].astype(BF16)
    wo2 = _pair_heads(wo[HG_WIDTH:], 0).astype(BF16)

    ohg, oatt_t = _mixer(sinks[0], x2, norm_mix_w[0][None, :], whg, watt, batt, lb_logits,
                         hg_norm_w[0][None, :], seq)
    out = _ffn(x2, ohg, oatt_t, wo1, wo2, norm_ffn_w[0][None, :],
               w_gate[0].astype(BF16), w_up[0].astype(BF16), conv_w[0], conv_b[0][None, :],
               w_down[0].astype(BF16), final_norm_w[None, :], batch, seq)
    return out.reshape(batch, seq, D_MODEL)
```

```python
import functools
import math

import jax
import jax.numpy as jnp
import numpy as np
from jax import lax
from jax.experimental import pallas as pl
from jax.experimental.pallas import tpu as pltpu

F32 = jnp.float32
BF16 = jnp.bfloat16

D_MODEL = 1024
HG_HEADS = 4
HG_DK = 128
HG_DV = 128
HG_WIDTH = HG_HEADS * HG_DK
ATT_HEADS = 8
ATT_KV_HEADS = 2
ATT_GROUP = ATT_HEADS // ATT_KV_HEADS
ATT_HD = 64
ATT_Q_WIDTH = ATT_HEADS * ATT_HD
ATT_KV_WIDTH = ATT_KV_HEADS * ATT_HD
ATT_BLOCK = 128
WINDOW = 128
HG_COLS = 4 * HG_WIDTH
ATT_COLS = ATT_Q_WIDTH + 2 * ATT_KV_WIDTH
D_FF = 2816
CONV_W = 3
EPS = 1e-6
LOG2E = 1.4426950408889634

LANES = 128
SUBLANES = 8
TILE = 512
HG_CHUNK = 128
HG_NV = HG_CHUNK // SUBLANES
HG_LEVELS = (8, 4, 2, 1)
PIECE = 256
HG_OUTPUT_LAG = 1
HG_STAGES_PER_PIECE = 3
HG_STAGES_PER_ATTN = 8
VMEM_LIMIT = 56 * 1024 * 1024

PF_COLS = 3 * HG_WIDTH
PB_AQ = HG_WIDTH
PB_AK = PB_AQ + ATT_Q_WIDTH
PB_AV = PB_AK + ATT_KV_WIDTH
PB_COLS = PB_AV + ATT_KV_WIDTH


def _dot(a, b):
    return jnp.dot(a, b, preferred_element_type=F32)


def _dot_nt(a, b):
    return lax.dot_general(a, b, (((1,), (1,)), ((), ())), preferred_element_type=F32)


def _dot_tn(a, b):
    return lax.dot_general(a, b, (((0,), (0,)), ((), ())), preferred_element_type=F32)


def _sigmoid(x):
    return 1.0 / (1.0 + jnp.exp2(x * (-LOG2E)))


def _inproj_pieces(x_ref, nw_ref, whg_ref, watt_ref, batt_ref, pf_ref, pb_ref):
    x = x_ref[...]
    inv = lax.rsqrt(jnp.mean(x * x, axis=-1, keepdims=True) + EPS)
    u = (x * inv * nw_ref[...]).astype(BF16)
    yield
    for j in range(HG_COLS // PIECE):
        c0 = j * PIECE
        p = _dot(u, whg_ref[:, c0:c0 + PIECE])
        if c0 < 2 * HG_WIDTH:
            pf_ref[:, c0:c0 + PIECE] = p
        elif c0 < 3 * HG_WIDTH:
            pb_ref[:, c0 - 2 * HG_WIDTH:c0 - 2 * HG_WIDTH + PIECE] = p.astype(BF16)
        else:
            pf_ref[:, c0 - HG_WIDTH:c0 - HG_WIDTH + PIECE] = p
        yield
    for j in range(ATT_COLS // PIECE):
        c0 = j * PIECE
        a = _dot(u, watt_ref[:, c0:c0 + PIECE]) + batt_ref[:, c0:c0 + PIECE]
        if c0 < ATT_Q_WIDTH:
            a = a * (1.0 / math.sqrt(ATT_HD))
        pb_ref[:, PB_AQ + c0:PB_AQ + c0 + PIECE] = a.astype(BF16)
        yield


def _rows(a):
    return [a[m * SUBLANES:(m + 1) * SUBLANES, :] for m in range(HG_NV)]


def _hgrn2_stages(pf_ref, pb_ref, lbl_ref, nw_ref, tri_ref, o_ref, st_ref, first):
    C = HG_CHUNK
    nch = TILE // C

    l = lbl_ref[...]
    e = jnp.exp(l - jnp.max(l, axis=0, keepdims=True))
    lb = e[0:1, :] / jnp.sum(e, axis=0, keepdims=True)

    f_all = lb + (1.0 - lb) * _sigmoid(pf_ref[:, HG_WIDTH:2 * HG_WIDTH])
    k_all = 1.0 - f_all
    g = jnp.log(f_all)
    g_hi = g.astype(BF16)
    g_lo = (g - g_hi.astype(F32)).astype(BF16)
    q_all = pf_ref[:, 0:HG_WIDTH]

    row = lax.broadcasted_iota(jnp.int32, (C, C), 0)
    col = lax.broadcasted_iota(jnp.int32, (C, C), 1)
    lag_id = jnp.where((row // SUBLANES) == (col // SUBLANES), row - col, -1).reshape(HG_NV, SUBLANES, C)
    half = C // 2
    urow = lax.broadcasted_iota(jnp.int32, (half, C), 0)
    ucol = lax.broadcasted_iota(jnp.int32, (half, C), 1)
    zero8 = jnp.zeros((SUBLANES, LANES), F32)
    zero_blk = jnp.zeros((C, LANES), BF16)
    cat = lambda xs, idx: jnp.concatenate([xs[m] for m in idx], axis=0)

    b_ch, scores_off = {}, {}
    nw = nw_ref[...]

    def pairs(ch):
        rs_ = slice(ch * C, (ch + 1) * C)
        b_all = _dot(tri_ref[...], jnp.concatenate([g_hi[rs_], g_lo[rs_]], axis=0)) * LOG2E
        b_ch[ch] = b_all
        scores_off[ch] = []
        for hp in range(0, HG_HEADS, 2):
            hsl = [slice(h * HG_DK, (h + 1) * HG_DK) for h in (hp, hp + 1)]
            bvs = [_rows(b_all[:, sl]) for sl in hsl]
            qvs = [_rows(q_all[rs_, sl]) for sl in hsl]
            kvs = [_rows(k_all[rs_, sl]) for sl in hsl]
            evs = [[jnp.broadcast_to(x[SUBLANES - 1:SUBLANES, :], (SUBLANES, LANES)) for x in bv]
                   for bv in bvs]
            srows = [[None] * HG_NV for _ in hsl]
            for lv in HG_LEVELS:
                nb = HG_NV // (2 * lv)
                lower = [i * 2 * lv + j for i in range(nb) for j in range(lv)]
                upper = [m + lv for m in lower]
                qhs, kfs = [], []
                for e_ in range(2):
                    r = jnp.concatenate([evs[e_][i * 2 * lv + lv - 1] for i in range(nb) for _ in range(lv)],
                                        axis=0)
                    kh = cat(kvs[e_], lower) * jnp.exp2(r - cat(bvs[e_], lower))
                    qhs.append((cat(qvs[e_], upper) * jnp.exp2(cat(bvs[e_], upper) - r)).astype(BF16))
                    khv = [kh[j * SUBLANES:(j + 1) * SUBLANES, :] for j in range(half // SUBLANES)]
                    kfull = [zero8] * HG_NV
                    for j, m in enumerate(lower):
                        kfull[m] = khv[j]
                    kfs.append(jnp.concatenate(kfull, axis=0).astype(BF16))
                rhs = jnp.concatenate([jnp.concatenate([kfs[0], zero_blk], axis=1),
                                       jnp.concatenate([zero_blk, kfs[1]], axis=1)], axis=0)
                s2 = _dot_nt(jnp.concatenate(qhs, axis=1), rhs)
                for e_ in range(2):
                    s = s2[:, e_ * C:(e_ + 1) * C]
                    if nb > 1:
                        blk = lv * SUBLANES
                        s = jnp.where((urow // blk) == (ucol // (2 * blk)), s, 0.0)
                    for j, m in enumerate(upper):
                        piece = s[j * SUBLANES:(j + 1) * SUBLANES, :]
                        srows[e_][m] = piece if srows[e_][m] is None else srows[e_][m] + piece
            for e_, sl in enumerate(hsl):
                srows[e_][0] = jnp.zeros((SUBLANES, C), F32)
                sc3 = jnp.concatenate(srows[e_], axis=0).reshape(HG_NV, SUBLANES, C)
                f3 = f_all[rs_, sl].reshape(HG_NV, SUBLANES, LANES)
                bd = k_all[rs_, sl].reshape(HG_NV, SUBLANES, LANES)
                q3 = q_all[rs_, sl].reshape(HG_NV, SUBLANES, LANES)
                for d in range(SUBLANES):
                    if d > 0:
                        bd = f3 * pltpu.roll(bd, 1, axis=1)
                    p = jnp.sum(q3 * bd, axis=-1, keepdims=True)
                    sc3 = jnp.where(lag_id == d, p, sc3)
                scores_off[ch].append(sc3.reshape(C, C))
                yield "vector"

    def outputs(ch):
        rs_ = slice(ch * C, (ch + 1) * C)
        for h in range(HG_HEADS):
            sl = slice(h * HG_DK, (h + 1) * HG_DK)
            b, q, kk = b_ch[ch][:, sl], q_all[rs_, sl], k_all[rs_, sl]
            vt = pb_ref[rs_, sl].astype(F32).T.astype(BF16)
            scores = scores_off[ch][h]
            st = st_ref[h]
            if ch == 0:
                st = jnp.where(first, 0.0, st)
            lhs = jnp.concatenate([scores.astype(BF16), (q * jnp.exp2(b)).astype(BF16)], axis=1)
            o = _dot_nt(lhs, jnp.concatenate([vt, st.astype(BF16)], axis=1))
            bl = b[C - 1:C, :]
            kt = (kk * jnp.exp2(bl - b)).astype(BF16)
            st_ref[h] = st * jnp.exp2(bl) + _dot(vt, kt)
            inv = lax.rsqrt(jnp.mean(o * o, axis=-1, keepdims=True) + EPS * HG_DK)
            hg = pf_ref[rs_, 2 * HG_WIDTH + h * HG_DK:2 * HG_WIDTH + (h + 1) * HG_DK]
            o_ref[rs_, sl] = (o * inv * nw * (hg * _sigmoid(hg))).astype(BF16)
            yield "matrix" if h == 1 else "other"

    for ch in range(nch + HG_OUTPUT_LAG):
        if ch < nch:
            yield from pairs(ch)
        if ch >= HG_OUTPUT_LAG:
            yield from outputs(ch - HG_OUTPUT_LAG)


def _cumsum_matrix():
    t = np.arange(HG_CHUNK)
    tri = (t[None, :] <= t[:, None]).astype(np.float32)
    return jnp.asarray(np.concatenate([tri, tri], axis=1), BF16)


def _attn_stages(sink_ref, bias_ref, pb_ref, kv_ref, o_ref, first):
    T = ATT_BLOCK
    G = ATT_GROUP
    ak = pb_ref[:, PB_AK:PB_AV]
    av = pb_ref[:, PB_AV:]
    k_all = jnp.concatenate([kv_ref[0], ak], axis=0)
    v_all = jnp.concatenate([kv_ref[1], av], axis=0)
    kv_ref[0] = ak[TILE - T:, :]
    kv_ref[1] = av[TILE - T:, :]
    vt_all = v_all.astype(F32).T.astype(BF16)
    lane = lax.broadcasted_iota(jnp.int32, (2 * T, LANES), 1)
    lo = (lane < ATT_HD).astype(F32).astype(BF16)
    hi = (lane >= ATT_HD).astype(F32).astype(BF16)
    rowi = lax.broadcasted_iota(jnp.int32, (LANES, 2 * T), 0)
    lo_r = (rowi < ATT_HD).astype(F32).astype(BF16)
    hi_r = (rowi >= ATT_HD).astype(F32).astype(BF16)
    lo_out = lax.broadcasted_iota(jnp.int32, (LANES, G * T), 0) < ATT_HD
    sink = [jnp.concatenate([jnp.full((1, T), sink_ref[g * G + c], F32) for c in range(G)], axis=1)
            for g in range(ATT_KV_HEADS)]
    for j in range(TILE // T):
        k = k_all[j * T:(j + 2) * T, :]
        vt = vt_all[:, j * T:(j + 2) * T]
        kcat = jnp.concatenate([k * lo, k * hi], axis=0)
        vcat_t = jnp.concatenate([vt * lo_r, vt * hi_r], axis=1)
        qs = jnp.concatenate([pb_ref[j * T:(j + 1) * T, PB_AQ + c * LANES:PB_AQ + (c + 1) * LANES]
                              for c in range(G)], axis=0)
        bias = bias_ref[jnp.where(first, 1, 0)] if j == 0 else bias_ref[0]
        bias = jnp.concatenate([bias] * G, axis=1)
        st = _dot_nt(kcat, qs)
        ps, dens = [], []
        for g in range(ATT_KV_HEADS):
            sg = st[g * 2 * T:(g + 1) * 2 * T, :] + bias
            m = jnp.maximum(jnp.max(sg, axis=0, keepdims=True), sink[g])
            p = jnp.exp(sg - m)
            dens.append(jnp.sum(p, axis=0, keepdims=True) + jnp.exp(sink[g] - m))
            ps.append(p.astype(BF16))
        od = _dot(vcat_t, jnp.concatenate(ps, axis=0))
        out = od / jnp.where(lo_out, dens[0], dens[1])
        for c in range(G):
            o_ref[c * LANES:(c + 1) * LANES, j * T:(j + 1) * T] = out[:, c * T:(c + 1) * T].astype(BF16)
        yield


def _attn_bias():
    T = ATT_BLOCK
    dist = np.arange(T)[None, :] + T - np.arange(2 * T)[:, None]
    band = (dist >= 0) & (dist < WINDOW)
    first = band & (np.arange(2 * T)[:, None] >= T)
    to_bias = lambda m: np.where(m, 0.0, -np.inf).astype(np.float32)
    return jnp.asarray(np.stack([to_bias(band), to_bias(first)]))


def _mixer_kernel(tiles_per_seq, sink_ref, x_ref, nw_ref, whg_ref, watt_ref, batt_ref, lbl_ref,
                  hnw_ref, tri_ref, bias_ref, ohg_ref, oatt_ref,
                  pf0_ref, pb0_ref, pf1_ref, pb1_ref, st_ref, kv_ref):
    t = pl.program_id(0)

    @pl.when(t == 0)
    def _():
        pf1_ref[...] = jnp.zeros_like(pf1_ref)
        pb1_ref[...] = jnp.zeros_like(pb1_ref)
        st_ref[...] = jnp.zeros_like(st_ref)
        kv_ref[...] = jnp.zeros_like(kv_ref)

    first = ((t + tiles_per_seq - 1) % tiles_per_seq) == 0

    def step(pf_in, pb_in, pf_out, pb_out):
        pieces = _inproj_pieces(x_ref, nw_ref, whg_ref, watt_ref, batt_ref, pf_out, pb_out)
        hgrn2 = _hgrn2_stages(pf_in, pb_in, lbl_ref, hnw_ref, tri_ref, ohg_ref, st_ref, first)
        attn = _attn_stages(sink_ref, bias_ref, pb_in, kv_ref, oatt_ref, first)
        next(pieces)
        for i, kind in enumerate(hgrn2):
            if i % HG_STAGES_PER_PIECE == 0:
                next(pieces, None)
            if i % HG_STAGES_PER_ATTN == HG_STAGES_PER_ATTN - 1:
                next(attn, None)
        for _ in attn:
            pass
        for _ in pieces:
            pass

    @pl.when(t % 2 == 0)
    def _():
        step(pf1_ref, pb1_ref, pf0_ref, pb0_ref)

    @pl.when(t % 2 == 1)
    def _():
        step(pf0_ref, pb0_ref, pf1_ref, pb1_ref)


def _mixer(sinks, x2, nw, whg, watt, batt, lbl, hnw, seq):
    n = x2.shape[0]
    nt = n // TILE
    tri2 = _cumsum_matrix()
    bias = _attn_bias()
    const = lambda a: pl.BlockSpec(a.shape, lambda t: (0,) * a.ndim, pipeline_mode=pl.Buffered(1))
    return pl.pallas_call(
        functools.partial(_mixer_kernel, seq // TILE),
        grid=(nt + 1,),
        in_specs=[pl.BlockSpec(memory_space=pltpu.SMEM),
                  pl.BlockSpec((TILE, D_MODEL), lambda t: (jnp.minimum(t, nt - 1), 0)),
                  const(nw), const(whg), const(watt), const(batt), const(lbl), const(hnw),
                  const(tri2), const(bias)],
        out_specs=[pl.BlockSpec((TILE, HG_WIDTH), lambda t: (jnp.maximum(t - 1, 0), 0)),
                   pl.BlockSpec((ATT_Q_WIDTH, TILE), lambda t: (0, jnp.maximum(t - 1, 0)))],
        out_shape=[jax.ShapeDtypeStruct((n, HG_WIDTH), BF16),
                   jax.ShapeDtypeStruct((ATT_Q_WIDTH, n), BF16)],
        scratch_shapes=[pltpu.VMEM((TILE, PF_COLS), F32), pltpu.VMEM((TILE, PB_COLS), BF16),
                        pltpu.VMEM((TILE, PF_COLS), F32), pltpu.VMEM((TILE, PB_COLS), BF16),
                        pltpu.VMEM((HG_HEADS, HG_DV, HG_DK), F32),
                        pltpu.VMEM((2, ATT_BLOCK, ATT_KV_WIDTH), BF16)],
        compiler_params=pltpu.CompilerParams(
            dimension_semantics=("arbitrary",), vmem_limit_bytes=VMEM_LIMIT),
        name="mixer",
    )(sinks, x2, nw, whg, watt, batt, lbl, hnw, tri2, bias)


FF_CHUNKS = ((0, 2816),)
FFN_TILE = 512
FFN_TILES = 1


def _ffn_kernel(x_ref, ohg_ref, oatt_t_ref, wo1_ref, wo2_ref, nfw_ref, wg_ref, wu_ref,
                cw_ref, cb_ref, wd_ref, fnw_ref, o_ref, tail_ref):
    tm = FFN_TILE

    @pl.when(pl.program_id(1) == 0)
    def _():
        tail_ref[...] = jnp.zeros_like(tail_ref)

    hs, vs = [], []
    for r in range(FFN_TILES):
        rows = slice(r * tm, (r + 1) * tm)
        h = (x_ref[rows, :] + _dot(ohg_ref[rows, :], wo1_ref[...])
             + _dot_tn(oatt_t_ref[:, rows], wo2_ref[...]))
        inv = lax.rsqrt(jnp.mean(h * h, axis=-1, keepdims=True) + EPS)
        hs.append(h)
        vs.append((h * inv * nfw_ref[...]).astype(BF16))
    tails = [tail_ref[:, c0:c0 + cw] for (c0, cw) in FF_CHUNKS]
    for r in range(FFN_TILES):
        rows = slice(r * tm, (r + 1) * tm)
        y = jnp.zeros((tm, D_MODEL), F32)
        for ci, (c0, cw) in enumerate(FF_CHUNKS):
            cs = slice(c0, c0 + cw)
            gp = _dot(vs[r], wg_ref[:, cs])
            up = _dot(vs[r], wu_ref[:, cs])
            gx = jnp.concatenate([tails[ci], gp], axis=0)
            tails[ci] = gp[tm - SUBLANES:, :]
            gate = (cw_ref[0:1, cs] * gx[SUBLANES - 2:SUBLANES - 2 + tm, :]
                    + cw_ref[1:2, cs] * gx[SUBLANES - 1:SUBLANES - 1 + tm, :]
                    + cw_ref[2:3, cs] * gp) + cb_ref[:, cs]
            act = (gate * _sigmoid(gate) * up).astype(BF16)
            y = y + _dot(act, wd_ref[cs, :])
        h2 = hs[r] + y
        inv2 = lax.rsqrt(jnp.mean(h2 * h2, axis=-1, keepdims=True) + EPS)
        o_ref[rows, :] = h2 * inv2 * fnw_ref[...]
    for ci, (c0, cw) in enumerate(FF_CHUNKS):
        tail_ref[:, c0:c0 + cw] = tails[ci]


def _ffn(x2, ohg, oatt_t, wo1, wo2, nfw, wg, wu, cw, cb, wd, fnw, batch, seq):
    n = x2.shape[0]
    tm = FFN_TILES * FFN_TILE
    assert seq % tm == 0
    nt = seq // tm
    row = lambda w: pl.BlockSpec((tm, w), lambda b, i: (b * nt + i, 0))
    const = lambda s: pl.BlockSpec(s, lambda b, i: (0, 0), pipeline_mode=pl.Buffered(1))
    return pl.pallas_call(
        _ffn_kernel,
        grid=(batch, nt),
        in_specs=[row(D_MODEL), row(HG_WIDTH),
                  pl.BlockSpec((ATT_Q_WIDTH, tm), lambda b, i: (0, b * nt + i)),
                  const(wo1.shape), const(wo2.shape), const(nfw.shape),
                  const(wg.shape), const(wu.shape), const(cw.shape), const(cb.shape),
                  const(wd.shape), const(fnw.shape)],
        out_specs=row(D_MODEL),
        out_shape=jax.ShapeDtypeStruct((n, D_MODEL), F32),
        scratch_shapes=[pltpu.VMEM((SUBLANES, D_FF), F32)],
        compiler_params=pltpu.CompilerParams(
            dimension_semantics=("arbitrary", "arbitrary"), vmem_limit_bytes=VMEM_LIMIT),
        name="ffn",
    )(x2, ohg, oatt_t, wo1, wo2, nfw, wg, wu, cw, cb, wd, fnw)


def _pair_heads(a, axis):
    shape = a.shape
    a = a.reshape(shape[:axis] + (ATT_KV_HEADS, ATT_GROUP, ATT_HD) + shape[axis + 1:])
    return jnp.swapaxes(a, axis, axis + 1).reshape(shape)


def kernel(x, norm_mix_w, w_in, b_attn, lb_logits, hg_norm_w, sinks, w_out, norm_ffn_w,
           w_gate, w_up, conv_w, conv_b, w_down, final_norm_w):
    batch, seq, _ = x.shape
    n = batch * seq
    assert seq % TILE == 0
    x2 = x.reshape(n, D_MODEL)

    w_in0 = w_in[0]
    whg = w_in0[:, :HG_COLS].astype(BF16)
    wq = _pair_heads(w_in0[:, HG_COLS:HG_COLS + ATT_Q_WIDTH], 1)
    watt = jnp.concatenate([wq, w_in0[:, HG_COLS + ATT_Q_WIDTH:]], axis=1).astype(BF16)
    ba = b_attn[0]
    batt = jnp.concatenate([_pair_heads(ba[:ATT_Q_WIDTH], 0), ba[ATT_Q_WIDTH:]])[None, :]
    wo = w_out[0]
    wo1 = wo[:HG_WIDTH].astype(BF16)
    wo2 = _pair_heads(wo[HG_WIDTH:], 0).astype(BF16)

    ohg, oatt_t = _mixer(sinks[0], x2, norm_mix_w[0][None, :], whg, watt, batt, lb_logits,
                         hg_norm_w[0][None, :], seq)
    out = _ffn(x2, ohg, oatt_t, wo1, wo2, norm_ffn_w[0][None, :],
               w_gate[0].astype(BF16), w_up[0].astype(BF16), conv_w[0], conv_b[0][None, :],
               w_down[0].astype(BF16), final_norm_w[None, :], batch, seq)
    return out.reshape(batch, seq, D_MODEL)
```

```python
import functools
import math

import jax
import jax.numpy as jnp
import numpy as np
from jax import lax
from jax.experimental import pallas as pl
from jax.experimental.pallas import tpu as pltpu

F32 = jnp.float32
BF16 = jnp.bfloat16

D_MODEL = 1024
HG_HEADS = 4
HG_DK = 128
HG_DV = 128
HG_WIDTH = HG_HEADS * HG_DK
ATT_HEADS = 8
ATT_KV_HEADS = 2
ATT_GROUP = ATT_HEADS // ATT_KV_HEADS
ATT_HD = 64
ATT_Q_WIDTH = ATT_HEADS * ATT_HD
ATT_KV_WIDTH = ATT_KV_HEADS * ATT_HD
ATT_BLOCK = 128
WINDOW = 128
HG_COLS = 4 * HG_WIDTH
ATT_COLS = ATT_Q_WIDTH + 2 * ATT_KV_WIDTH
D_FF = 2816
CONV_W = 3
EPS = 1e-6
LOG2E = 1.4426950408889634

LANES = 128
SUBLANES = 8
TILE = 512
HG_CHUNK = 128
HG_NV = HG_CHUNK // SUBLANES
HG_LEVELS = (8, 4, 2, 1)
PIECE = 256
HG_STAGES_PER_PIECE = 3
HG_STAGES_PER_ATTN = 8
VMEM_LIMIT = 56 * 1024 * 1024

PF_COLS = 3 * HG_WIDTH
PB_AQ = HG_WIDTH
PB_AK = PB_AQ + ATT_Q_WIDTH
PB_AV = PB_AK + ATT_KV_WIDTH
PB_COLS = PB_AV + ATT_KV_WIDTH


def _dot(a, b):
    return jnp.dot(a, b, preferred_element_type=F32)


def _dot_nt(a, b):
    return lax.dot_general(a, b, (((1,), (1,)), ((), ())), preferred_element_type=F32)


def _dot_tn(a, b):
    return lax.dot_general(a, b, (((0,), (0,)), ((), ())), preferred_element_type=F32)


def _sigmoid(x):
    return 1.0 / (1.0 + jnp.exp2(x * (-LOG2E)))


def _inproj_pieces(x_ref, nw_ref, whg_ref, watt_ref, batt_ref, pf_ref, pb_ref):
    x = x_ref[...]
    inv = lax.rsqrt(jnp.mean(x * x, axis=-1, keepdims=True) + EPS)
    u = (x * inv * nw_ref[...]).astype(BF16)
    yield
    for c0 in range(0, HG_COLS, PIECE):
        p = _dot(u, whg_ref[:, c0:c0 + PIECE])
        if c0 < 2 * HG_WIDTH:
            pf_ref[:, c0:c0 + PIECE] = p
        elif c0 < 3 * HG_WIDTH:
            pb_ref[:, c0 - 2 * HG_WIDTH:c0 - 2 * HG_WIDTH + PIECE] = p.astype(BF16)
        else:
            pf_ref[:, c0 - HG_WIDTH:c0 - HG_WIDTH + PIECE] = p
        yield
    for c0 in range(0, ATT_COLS, PIECE):
        a = _dot(u, watt_ref[:, c0:c0 + PIECE]) + batt_ref[:, c0:c0 + PIECE]
        if c0 < ATT_Q_WIDTH:
            a = a * (1.0 / math.sqrt(ATT_HD))
        pb_ref[:, PB_AQ + c0:PB_AQ + c0 + PIECE] = a.astype(BF16)
        yield


def _rows(a):
    return [a[m * SUBLANES:(m + 1) * SUBLANES, :] for m in range(HG_NV)]


def _hgrn2_stages(pf_ref, pb_ref, lbl_ref, nw_ref, tri_ref, o_ref, st_ref, first):
    C = HG_CHUNK
    nch = TILE // C

    l = lbl_ref[...]
    e = jnp.exp(l - jnp.max(l, axis=0, keepdims=True))
    lb = e[0:1, :] / jnp.sum(e, axis=0, keepdims=True)

    f_all = lb + (1.0 - lb) * _sigmoid(pf_ref[:, HG_WIDTH:2 * HG_WIDTH])
    k_all = 1.0 - f_all
    g = jnp.log(f_all)
    g_hi = g.astype(BF16)
    g_lo = (g - g_hi.astype(F32)).astype(BF16)
    q_all = pf_ref[:, 0:HG_WIDTH]

    row = lax.broadcasted_iota(jnp.int32, (C, C), 0)
    col = lax.broadcasted_iota(jnp.int32, (C, C), 1)
    lag_id = jnp.where((row // SUBLANES) == (col // SUBLANES), row - col, -1).reshape(HG_NV, SUBLANES, C)
    half = C // 2
    urow = lax.broadcasted_iota(jnp.int32, (half, C), 0)
    ucol = lax.broadcasted_iota(jnp.int32, (half, C), 1)
    zero8 = jnp.zeros((SUBLANES, LANES), F32)
    zero_blk = jnp.zeros((C, LANES), BF16)
    cat = lambda xs, idx: jnp.concatenate([xs[m] for m in idx], axis=0)

    b_ch, scores = {}, {}
    nw = nw_ref[...]

    def pairs(ch):
        rs_ = slice(ch * C, (ch + 1) * C)
        b_all = _dot(tri_ref[...], jnp.concatenate([g_hi[rs_], g_lo[rs_]], axis=0)) * LOG2E
        b_ch[ch] = b_all
        scores[ch] = []
        for hp in range(0, HG_HEADS, 2):
            hsl = [slice(h * HG_DK, (h + 1) * HG_DK) for h in (hp, hp + 1)]
            bvs = [_rows(b_all[:, sl]) for sl in hsl]
            qvs = [_rows(q_all[rs_, sl]) for sl in hsl]
            kvs = [_rows(k_all[rs_, sl]) for sl in hsl]
            evs = [[jnp.broadcast_to(x[SUBLANES - 1:SUBLANES, :], (SUBLANES, LANES)) for x in bv]
                   for bv in bvs]
            srows = [[None] * HG_NV for _ in hsl]
            for lv in HG_LEVELS:
                nb = HG_NV // (2 * lv)
                lower = [i * 2 * lv + j for i in range(nb) for j in range(lv)]
                upper = [m + lv for m in lower]
                qhs, kfs = [], []
                for e_ in range(2):
                    r = jnp.concatenate([evs[e_][i * 2 * lv + lv - 1] for i in range(nb) for _ in range(lv)],
                                        axis=0)
                    kh = cat(kvs[e_], lower) * jnp.exp2(r - cat(bvs[e_], lower))
                    qhs.append((cat(qvs[e_], upper) * jnp.exp2(cat(bvs[e_], upper) - r)).astype(BF16))
                    khv = [kh[j * SUBLANES:(j + 1) * SUBLANES, :] for j in range(half // SUBLANES)]
                    kfull = [zero8] * HG_NV
                    for j, m in enumerate(lower):
                        kfull[m] = khv[j]
                    kfs.append(jnp.concatenate(kfull, axis=0).astype(BF16))
                rhs = jnp.concatenate([jnp.concatenate([kfs[0], zero_blk], axis=1),
                                       jnp.concatenate([zero_blk, kfs[1]], axis=1)], axis=0)
                s2 = _dot_nt(jnp.concatenate(qhs, axis=1), rhs)
                for e_ in range(2):
                    s = s2[:, e_ * C:(e_ + 1) * C]
                    if nb > 1:
                        blk = lv * SUBLANES
                        s = jnp.where((urow // blk) == (ucol // (2 * blk)), s, 0.0)
                    for j, m in enumerate(upper):
                        piece = s[j * SUBLANES:(j + 1) * SUBLANES, :]
                        srows[e_][m] = piece if srows[e_][m] is None else srows[e_][m] + piece
            for e_, sl in enumerate(hsl):
                srows[e_][0] = jnp.zeros((SUBLANES, C), F32)
                sc3 = jnp.concatenate(srows[e_], axis=0).reshape(HG_NV, SUBLANES, C)
                f3 = f_all[rs_, sl].reshape(HG_NV, SUBLANES, LANES)
                bd = k_all[rs_, sl].reshape(HG_NV, SUBLANES, LANES)
                q3 = q_all[rs_, sl].reshape(HG_NV, SUBLANES, LANES)
                for d in range(SUBLANES):
                    if d > 0:
                        bd = f3 * pltpu.roll(bd, 1, axis=1)
                    p = jnp.sum(q3 * bd, axis=-1, keepdims=True)
                    sc3 = jnp.where(lag_id == d, p, sc3)
                scores[ch].append(sc3.reshape(C, C))
                yield

    def outputs(ch):
        rs_ = slice(ch * C, (ch + 1) * C)
        for h in range(HG_HEADS):
            sl = slice(h * HG_DK, (h + 1) * HG_DK)
            b, q, kk = b_ch[ch][:, sl], q_all[rs_, sl], k_all[rs_, sl]
            vt = pb_ref[rs_, sl].astype(F32).T.astype(BF16)
            st = st_ref[h]
            if ch == 0:
                st = jnp.where(first, 0.0, st)
            lhs = jnp.concatenate([scores[ch][h].astype(BF16), (q * jnp.exp2(b)).astype(BF16)], axis=1)
            o = _dot_nt(lhs, jnp.concatenate([vt, st.astype(BF16)], axis=1))
            bl = b[C - 1:C, :]
            kt = (kk * jnp.exp2(bl - b)).astype(BF16)
            st_ref[h] = st * jnp.exp2(bl) + _dot(vt, kt)
            inv = lax.rsqrt(jnp.mean(o * o, axis=-1, keepdims=True) + EPS * HG_DK)
            hg = pf_ref[rs_, 2 * HG_WIDTH + h * HG_DK:2 * HG_WIDTH + (h + 1) * HG_DK]
            o_ref[rs_, sl] = (o * inv * nw * (hg * _sigmoid(hg))).astype(BF16)
            yield

    for ch in range(nch + 1):
        if ch < nch:
            yield from pairs(ch)
        if ch >= 1:
            yield from outputs(ch - 1)


def _cumsum_matrix():
    t = np.arange(HG_CHUNK)
    tri = (t[None, :] <= t[:, None]).astype(np.float32)
    return jnp.asarray(np.concatenate([tri, tri], axis=1), BF16)


def _attn_stages(sink_ref, bias_ref, pb_ref, kv_ref, o_ref, first):
    T = ATT_BLOCK
    G = ATT_GROUP
    ak = pb_ref[:, PB_AK:PB_AV]
    av = pb_ref[:, PB_AV:]
    k_all = jnp.concatenate([kv_ref[0], ak], axis=0)
    v_all = jnp.concatenate([kv_ref[1], av], axis=0)
    kv_ref[0] = ak[TILE - T:, :]
    kv_ref[1] = av[TILE - T:, :]
    vt_all = v_all.astype(F32).T.astype(BF16)
    lane = lax.broadcasted_iota(jnp.int32, (2 * T, LANES), 1)
    lo = (lane < ATT_HD).astype(F32).astype(BF16)
    hi = (lane >= ATT_HD).astype(F32).astype(BF16)
    rowi = lax.broadcasted_iota(jnp.int32, (LANES, 2 * T), 0)
    lo_r = (rowi < ATT_HD).astype(F32).astype(BF16)
    hi_r = (rowi >= ATT_HD).astype(F32).astype(BF16)
    lo_out = lax.broadcasted_iota(jnp.int32, (LANES, G * T), 0) < ATT_HD
    sink = [jnp.concatenate([jnp.full((1, T), sink_ref[g * G + c], F32) for c in range(G)], axis=1)
            for g in range(ATT_KV_HEADS)]
    for j in range(TILE // T):
        k = k_all[j * T:(j + 2) * T, :]
        vt = vt_all[:, j * T:(j + 2) * T]
        kcat = jnp.concatenate([k * lo, k * hi], axis=0)
        vcat_t = jnp.concatenate([vt * lo_r, vt * hi_r], axis=1)
        qs = jnp.concatenate([pb_ref[j * T:(j + 1) * T, PB_AQ + c * LANES:PB_AQ + (c + 1) * LANES]
                              for c in range(G)], axis=0)
        bias = bias_ref[jnp.where(first, 1, 0)] if j == 0 else bias_ref[0]
        bias = jnp.concatenate([bias] * G, axis=1)
        st = _dot_nt(kcat, qs)
        ps, dens = [], []
        for g in range(ATT_KV_HEADS):
            sg = st[g * 2 * T:(g + 1) * 2 * T, :] + bias
            m = jnp.maximum(jnp.max(sg, axis=0, keepdims=True), sink[g])
            p = jnp.exp(sg - m)
            dens.append(jnp.sum(p, axis=0, keepdims=True) + jnp.exp(sink[g] - m))
            ps.append(p.astype(BF16))
        od = _dot(vcat_t, jnp.concatenate(ps, axis=0))
        out = od / jnp.where(lo_out, dens[0], dens[1])
        for c in range(G):
            o_ref[c * LANES:(c + 1) * LANES, j * T:(j + 1) * T] = out[:, c * T:(c + 1) * T].astype(BF16)
        yield


def _attn_bias():
    T = ATT_BLOCK
    dist = np.arange(T)[None, :] + T - np.arange(2 * T)[:, None]
    band = (dist >= 0) & (dist < WINDOW)
    first = band & (np.arange(2 * T)[:, None] >= T)
    to_bias = lambda m: np.where(m, 0.0, -np.inf).astype(np.float32)
    return jnp.asarray(np.stack([to_bias(band), to_bias(first)]))


def _mixer_kernel(tiles_per_seq, sink_ref, x_ref, nw_ref, whg_ref, watt_ref, batt_ref, lbl_ref,
                  hnw_ref, tri_ref, bias_ref, ohg_ref, oatt_ref,
                  pf0_ref, pb0_ref, pf1_ref, pb1_ref, st_ref, kv_ref):
    t = pl.program_id(0)

    @pl.when(t == 0)
    def _():
        pf1_ref[...] = jnp.zeros_like(pf1_ref)
        pb1_ref[...] = jnp.zeros_like(pb1_ref)
        st_ref[...] = jnp.zeros_like(st_ref)
        kv_ref[...] = jnp.zeros_like(kv_ref)

    first = ((t + tiles_per_seq - 1) % tiles_per_seq) == 0

    def step(pf_in, pb_in, pf_out, pb_out):
        pieces = _inproj_pieces(x_ref, nw_ref, whg_ref, watt_ref, batt_ref, pf_out, pb_out)
        hgrn2 = _hgrn2_stages(pf_in, pb_in, lbl_ref, hnw_ref, tri_ref, ohg_ref, st_ref, first)
        attn = _attn_stages(sink_ref, bias_ref, pb_in, kv_ref, oatt_ref, first)
        next(pieces)
        for i, _ in enumerate(hgrn2):
            if i % HG_STAGES_PER_PIECE == 0:
                next(pieces, None)
            if i % HG_STAGES_PER_ATTN == HG_STAGES_PER_ATTN - 1:
                next(attn, None)
        for _ in attn:
            pass
        for _ in pieces:
            pass

    @pl.when(t % 2 == 0)
    def _():
        step(pf1_ref, pb1_ref, pf0_ref, pb0_ref)

    @pl.when(t % 2 == 1)
    def _():
        step(pf0_ref, pb0_ref, pf1_ref, pb1_ref)


def _mixer(sinks, x2, nw, whg, watt, batt, lbl, hnw, seq):
    n = x2.shape[0]
    nt = n // TILE
    tri2 = _cumsum_matrix()
    bias = _attn_bias()
    const = lambda a: pl.BlockSpec(a.shape, lambda t: (0,) * a.ndim, pipeline_mode=pl.Buffered(1))
    return pl.pallas_call(
        functools.partial(_mixer_kernel, seq // TILE),
        grid=(nt + 1,),
        in_specs=[pl.BlockSpec(memory_space=pltpu.SMEM),
                  pl.BlockSpec((TILE, D_MODEL), lambda t: (jnp.minimum(t, nt - 1), 0)),
                  const(nw), const(whg), const(watt), const(batt), const(lbl), const(hnw),
                  const(tri2), const(bias)],
        out_specs=[pl.BlockSpec((TILE, HG_WIDTH), lambda t: (jnp.maximum(t - 1, 0), 0)),
                   pl.BlockSpec((ATT_Q_WIDTH, TILE), lambda t: (0, jnp.maximum(t - 1, 0)))],
        out_shape=[jax.ShapeDtypeStruct((n, HG_WIDTH), BF16),
                   jax.ShapeDtypeStruct((ATT_Q_WIDTH, n), BF16)],
        scratch_shapes=[pltpu.VMEM((TILE, PF_COLS), F32), pltpu.VMEM((TILE, PB_COLS), BF16),
                        pltpu.VMEM((TILE, PF_COLS), F32), pltpu.VMEM((TILE, PB_COLS), BF16),
                        pltpu.VMEM((HG_HEADS, HG_DV, HG_DK), F32),
                        pltpu.VMEM((2, ATT_BLOCK, ATT_KV_WIDTH), BF16)],
        compiler_params=pltpu.CompilerParams(
            dimension_semantics=("arbitrary",), vmem_limit_bytes=VMEM_LIMIT),
        name="mixer",
    )(sinks, x2, nw, whg, watt, batt, lbl, hnw, tri2, bias)


def _ffn_kernel(x_ref, ohg_ref, oatt_t_ref, wo1_ref, wo2_ref, nfw_ref, wgu_ref,
                cw_ref, cb_ref, wd_ref, fnw_ref, o_ref, tail_ref):
    tm = TILE

    @pl.when(pl.program_id(1) == 0)
    def _():
        tail_ref[...] = jnp.zeros_like(tail_ref)

    h = x_ref[...] + _dot(ohg_ref[...], wo1_ref[...]) + _dot_tn(oatt_t_ref[...], wo2_ref[...])
    inv = lax.rsqrt(jnp.mean(h * h, axis=-1, keepdims=True) + EPS)
    v = (h * inv * nfw_ref[...]).astype(BF16)
    gu = _dot(v, wgu_ref[...])
    gp = gu[:, :D_FF]
    gx = jnp.concatenate([tail_ref[...], gp], axis=0)
    tail_ref[...] = gp[tm - SUBLANES:, :]
    gate = (cw_ref[0:1, :] * gx[SUBLANES - 2:SUBLANES - 2 + tm, :]
            + cw_ref[1:2, :] * gx[SUBLANES - 1:SUBLANES - 1 + tm, :]
            + cw_ref[2:3, :] * gp) + cb_ref[...]
    act = (gate * _sigmoid(gate) * gu[:, D_FF:]).astype(BF16)
    h2 = h + _dot(act, wd_ref[...])
    inv2 = lax.rsqrt(jnp.mean(h2 * h2, axis=-1, keepdims=True) + EPS)
    o_ref[...] = h2 * inv2 * fnw_ref[...]


def _ffn(x2, ohg, oatt_t, wo1, wo2, nfw, wgu, cw, cb, wd, fnw, batch, seq):
    n = x2.shape[0]
    tm = TILE
    nt = seq // tm
    row = lambda w: pl.BlockSpec((tm, w), lambda b, i: (b * nt + i, 0))
    const = lambda s: pl.BlockSpec(s, lambda b, i: (0, 0), pipeline_mode=pl.Buffered(1))
    return pl.pallas_call(
        _ffn_kernel,
        grid=(batch, nt),
        in_specs=[row(D_MODEL), row(HG_WIDTH),
                  pl.BlockSpec((ATT_Q_WIDTH, tm), lambda b, i: (0, b * nt + i)),
                  const(wo1.shape), const(wo2.shape), const(nfw.shape),
                  const(wgu.shape), const(cw.shape), const(cb.shape),
                  const(wd.shape), const(fnw.shape)],
        out_specs=row(D_MODEL),
        out_shape=jax.ShapeDtypeStruct((n, D_MODEL), F32),
        scratch_shapes=[pltpu.VMEM((SUBLANES, D_FF), F32)],
        compiler_params=pltpu.CompilerParams(
            dimension_semantics=("arbitrary", "arbitrary"), vmem_limit_bytes=VMEM_LIMIT),
        name="ffn",
    )(x2, ohg, oatt_t, wo1, wo2, nfw, wgu, cw, cb, wd, fnw)


def _pair_heads(a, axis):
    shape = a.shape
    a = a.reshape(shape[:axis] + (ATT_KV_HEADS, ATT_GROUP, ATT_HD) + shape[axis + 1:])
    return jnp.swapaxes(a, axis, axis + 1).reshape(shape)


def kernel(x, norm_mix_w, w_in, b_attn, lb_logits, hg_norm_w, sinks, w_out, norm_ffn_w,
           w_gate, w_up, conv_w, conv_b, w_down, final_norm_w):
    batch, seq, _ = x.shape
    n = batch * seq
    assert seq % TILE == 0
    x2 = x.reshape(n, D_MODEL)

    w_in0 = w_in[0]
    whg = w_in0[:, :HG_COLS].astype(BF16)
    wq = _pair_heads(w_in0[:, HG_COLS:HG_COLS + ATT_Q_WIDTH], 1)
    watt = jnp.concatenate([wq, w_in0[:, HG_COLS + ATT_Q_WIDTH:]], axis=1).astype(BF16)
    ba = b_attn[0]
    batt = jnp.concatenate([_pair_heads(ba[:ATT_Q_WIDTH], 0), ba[ATT_Q_WIDTH:]])[None, :]
    wo = w_out[0]
    wo1 = wo[:HG_WIDTH].astype(BF16)
    wo2 = _pair_heads(wo[HG_WIDTH:], 0).astype(BF16)
    wgu = jnp.concatenate([w_gate[0], w_up[0]], axis=1).astype(BF16)

    ohg, oatt_t = _mixer(sinks[0], x2, norm_mix_w[0][None, :], whg, watt, batt, lb_logits,
                         hg_norm_w[0][None, :], seq)
    out = _ffn(x2, ohg, oatt_t, wo1, wo2, norm_ffn_w[0][None, :], wgu, conv_w[0],
               conv_b[0][None, :], w_down[0].astype(BF16), final_norm_w[None, :], batch, seq)
    return out.reshape(batch, seq, D_MODEL)
```
